```python
import math
import jax, jax.numpy as jnp
from jax import lax
import numpy as np


D_MODEL = 1024
BATCH = 8
SEQ = 4096
DEPTH = 4

CHUNK = 64
N_META = 16
Q_BLOCK = 128
D_MIX = D_MODEL
NORM_EPS = 1e-6
SUBLN_EPS = 1e-5
NEG = -1e30

FOX_HEADS = 4
FOX_HD = D_MODEL // 16
FOX_W = FOX_HEADS * FOX_HD
FORGET_BIAS_CENTER = 3.0

DIFF_HEADS = 4
DIFF_HD = D_MODEL // 16
DIFF_VD = 2 * DIFF_HD
DIFF_W = DIFF_HEADS * DIFF_VD
ALIBI_SLOPES = tuple(2.0 ** (-8.0 * (h + 1) / DIFF_HEADS) for h in range(DIFF_HEADS))

POOL_WINDOWS = (2, 4, 8, 16)
POOL_GROUPS = len(POOL_WINDOWS)
POOL_W = D_MIX - FOX_W - DIFF_W
POOL_GD = POOL_W // POOL_GROUPS

IN_SIZES = (FOX_W, FOX_W, FOX_W, FOX_W, FOX_HEADS,
            2 * DIFF_HEADS * DIFF_HD, 2 * DIFF_HEADS * DIFF_HD, DIFF_W, DIFF_W,
            POOL_W, POOL_W)
IN_DIM = sum(IN_SIZES)
IN_OFFSETS = tuple(int(o) for o in np.cumsum(IN_SIZES)[:-1])

kernel_name = 'hymba_fox_diff_pool_chunk_causal'


def rmsnorm(x, g, eps=NORM_EPS):
    xf = x.astype(jnp.float32)
    y = xf * lax.rsqrt(jnp.mean(xf * xf, axis=-1, keepdims=True) + eps) * g.astype(jnp.float32)
    return y.astype(x.dtype)


def chunk_ids(pos):
    return jnp.where(pos < N_META, 0, 1 + (pos - N_META) // CHUNK)


def chunk_end(p):
    return N_META + CHUNK * ((p - N_META) // CHUNK + 1)


def forgetting_attention(q, k, v, c):
    Lp = q.shape[1]
    pos = jnp.arange(Lp)
    scale = FOX_HD ** -0.5
    outs = []
    for q0 in range(0, Lp, Q_BLOCK):
        q1 = q0 + Q_BLOCK
        kend = q1
        s = jnp.einsum('bqhd,bkhd->bhqk', q[:, q0:q1], k[:, :kend]).astype(jnp.float32) * scale
        s = s + c[:, :, q0:q1, None] - c[:, :, None, :kend]
        mask = pos[None, :kend] <= pos[q0:q1, None]
        p = jax.nn.softmax(jnp.where(mask, s, NEG), axis=-1).astype(v.dtype)
        outs.append(jnp.einsum('bhqk,bkhd->bqhd', p, v[:, :kend]))
    return jnp.concatenate(outs, axis=1)


def differential_attention(q, k, v, lam):
    Lp = q.shape[1]
    pos = jnp.arange(Lp)
    chunk = chunk_ids(pos)
    slopes = jnp.asarray(ALIBI_SLOPES, jnp.float32)
    scale = DIFF_HD ** -0.5
    outs = []
    for q0 in range(0, Lp, Q_BLOCK):
        q1 = q0 + Q_BLOCK
        kend = min(Lp, chunk_end(q1 - 1))
        s = jnp.einsum('bqhcd,bkhcd->bchqk', q[:, q0:q1], k[:, :kend]).astype(jnp.float32) * scale
        dist = jnp.abs(pos[q0:q1, None] - pos[None, :kend]).astype(jnp.float32)
        s = s - slopes[:, None, None] * dist
        mask = chunk[None, :kend] <= chunk[q0:q1, None]
        p = jax.nn.softmax(jnp.where(mask, s, NEG), axis=-1)
        w = (p[:, 0] - lam * p[:, 1]).astype(v.dtype)
        outs.append(jnp.einsum('bhqk,bkhe->bqhe', w, v[:, :kend]))
    return jnp.concatenate(outs, axis=1)


def pooling_mixer(u, w_pool, pool_scale):
    B, Lp, _ = u.shape
    uf = u.astype(jnp.float32).reshape(B, Lp, POOL_GROUPS, POOL_GD)
    cs = jnp.cumsum(uf, axis=1)
    cs0 = jnp.pad(cs, ((0, 0), (1, 0), (0, 0), (0, 0)))
    t = jnp.arange(Lp)
    win = jnp.asarray(POOL_WINDOWS)
    lo = jnp.maximum(t[:, None] + 1 - win[None, :], 0)
    lower = cs0[:, lo, jnp.arange(POOL_GROUPS)[None, :]]
    cnt = (t[:, None] + 1 - lo).astype(jnp.float32)
    pooled = (cs - lower) / cnt[None, :, :, None] - uf
    y = jnp.einsum('blgc,gcd->blgd', pooled.astype(u.dtype), w_pool)
    return y.reshape(B, Lp, POOL_W) * pool_scale


def setup_inputs(seed: int = 0) -> dict:
    key = jax.random.key(seed)
    ks = jax.random.split(key, 14)
    f32 = jnp.float32
    x = jax.random.normal(ks[0], (BATCH, SEQ, D_MODEL), f32)
    meta_tokens = jax.random.normal(ks[1], (N_META, D_MODEL), f32)
    norm_g = 1.0 + 0.02 * jax.random.normal(ks[2], (DEPTH, D_MODEL), f32)
    w_in = jax.random.normal(ks[3], (DEPTH, D_MODEL, IN_DIM), f32) * D_MODEL ** -0.5
    b_f = FORGET_BIAS_CENTER + 0.5 * jax.random.normal(ks[4], (DEPTH, FOX_HEADS), f32)
    lam_q1 = 0.1 * jax.random.normal(ks[5], (DEPTH, DIFF_HD), f32)
    lam_k1 = 0.1 * jax.random.normal(ks[6], (DEPTH, DIFF_HD), f32)
    lam_q2 = 0.1 * jax.random.normal(ks[7], (DEPTH, DIFF_HD), f32)
    lam_k2 = 0.1 * jax.random.normal(ks[8], (DEPTH, DIFF_HD), f32)
    subln_g = 1.0 + 0.02 * jax.random.normal(ks[9], (DEPTH, DIFF_VD), f32)
    w_pool = jax.random.normal(ks[10], (DEPTH, POOL_GROUPS, POOL_GD, POOL_GD), f32) * POOL_GD ** -0.5
    pool_scale = 1.0 + 0.1 * jax.random.normal(ks[11], (DEPTH, POOL_W), f32)
    w_out = jax.random.normal(ks[12], (DEPTH, D_MIX, D_MODEL), f32) * D_MIX ** -0.5
    final_g = 1.0 + 0.02 * jax.random.normal(ks[13], (D_MODEL,), f32)
    return {'x': x, 'meta_tokens': meta_tokens, 'norm_g': norm_g, 'w_in': w_in, 'b_f': b_f,
            'lam_q1': lam_q1, 'lam_k1': lam_k1, 'lam_q2': lam_q2, 'lam_k2': lam_k2,
            'subln_g': subln_g, 'w_pool': w_pool, 'pool_scale': pool_scale,
            'w_out': w_out, 'final_g': final_g}


def reference(x, meta_tokens, norm_g, w_in, b_f, lam_q1, lam_k1, lam_q2, lam_k2,
              subln_g, w_pool, pool_scale, w_out, final_g):
    B, S, D = x.shape
    L = N_META + S
    Lp = -(-L // Q_BLOCK) * Q_BLOCK
    meta = jnp.broadcast_to(meta_tokens.astype(x.dtype)[None], (B, N_META, D))
    h = jnp.concatenate([meta, x], axis=1)
    h = jnp.pad(h, ((0, 0), (0, Lp - L), (0, 0)))
    for l in range(DEPTH):
        hn = rmsnorm(h, norm_g[l])
        proj = jnp.einsum('bld,de->ble', hn, w_in[l])
        fq, fk, fv, fz, fg, dq, dk, dv, dz, pu, pz = jnp.split(proj, IN_OFFSETS, axis=-1)

        log_f = jax.nn.log_sigmoid(fg.astype(jnp.float32) + b_f[l].astype(jnp.float32))
        c = jnp.transpose(jnp.cumsum(log_f, axis=1), (0, 2, 1))
        a_out = forgetting_attention(fq.reshape(B, Lp, FOX_HEADS, FOX_HD),
                                     fk.reshape(B, Lp, FOX_HEADS, FOX_HD),
                                     fv.reshape(B, Lp, FOX_HEADS, FOX_HD), c)
        a_out = a_out.reshape(B, Lp, FOX_W) * jax.nn.silu(fz)

        lambda_init = 0.8 - 0.6 * math.exp(-0.3 * l)
        lam = (jnp.exp(jnp.sum(lam_q1[l].astype(jnp.float32) * lam_k1[l].astype(jnp.float32)))
               - jnp.exp(jnp.sum(lam_q2[l].astype(jnp.float32) * lam_k2[l].astype(jnp.float32)))
               + lambda_init)
        b_out = differential_attention(dq.reshape(B, Lp, DIFF_HEADS, 2, DIFF_HD),
                                       dk.reshape(B, Lp, DIFF_HEADS, 2, DIFF_HD),
                                       dv.reshape(B, Lp, DIFF_HEADS, DIFF_VD), lam)
        b_out = rmsnorm(b_out, subln_g[l], SUBLN_EPS) * (1.0 - lambda_init)
        b_out = b_out.reshape(B, Lp, DIFF_W) * jax.nn.silu(dz)

        c_out = pooling_mixer(pu, w_pool[l], pool_scale[l]) * jax.nn.silu(pz)

        mix = jnp.concatenate([a_out, b_out, c_out], axis=-1)
        h = h + jnp.einsum('ble,ed->bld', mix, w_out[l])
    y = rmsnorm(h[:, N_META:N_META + S], final_g)
    return y
```

```python
import functools
import math

import jax
import jax.numpy as jnp
from jax import lax
from jax.experimental import pallas as pl
from jax.experimental.pallas import tpu as pltpu

F32 = jnp.float32
BF16 = jnp.bfloat16

LANE = 128
N_META = 16
CHUNK = 64
FRONT = LANE - N_META
HEAD = 64
FOX_HEADS = 4
DIFF_HEADS = 4
FOX_W = 256
DIFF_W = 512
POOL_W = 256
POOL_WINDOWS = (2, 4, 8, 16)
NORM_EPS = 1e-6
SUBLN_EPS = 1e-5
MASK = -1e30
M_INIT = -1e38
FORGET_PAD = LANE
TQ = 256
VMEM_LIMIT = 56 * 1024 * 1024

_IN_SIZES = (FOX_W, FOX_W, FOX_W, FOX_W, FOX_HEADS, DIFF_W, DIFF_W, DIFF_W, DIFF_W, POOL_W, POOL_W)
_IN_OFF = tuple(int(sum(_IN_SIZES[:i])) for i in range(len(_IN_SIZES) + 1))
P1_W = 4 * FOX_W
P2_W = 4 * DIFF_W
P3_W = 2 * POOL_W
W_COLS = P1_W + P2_W + P3_W + FORGET_PAD


def _split3(x):
    hi = x.astype(BF16)
    r1 = x - hi.astype(F32)
    mid = r1.astype(BF16)
    lo = (r1 - mid.astype(F32)).astype(BF16)
    return hi, mid, lo


def _proj_kernel(h_ref, g_ref, w_ref, bf_ref, wp_ref, ps_ref,
                 p1_ref, p2_ref, cf_ref, cp_ref, carry_ref, u_ref):
    i = pl.program_id(1)
    tm = h_ref.shape[1]
    halo = POOL_WINDOWS[-1]

    @pl.when(i == 0)
    def _():
        carry_ref[...] = jnp.zeros_like(carry_ref)
        u_ref[0:halo, :] = jnp.zeros((halo, POOL_W), F32)

    h = h_ref[0]
    hn = h * lax.rsqrt(jnp.mean(h * h, axis=-1, keepdims=True) + NORM_EPS) * g_ref[...]
    proj = jnp.dot(hn.astype(BF16), w_ref[...], preferred_element_type=F32)
    p1_ref[0] = proj[:, 0:P1_W].astype(BF16)
    p2_ref[0] = proj[:, P1_W:P1_W + P2_W].astype(BF16)
    pu = proj[:, P1_W + P2_W:P1_W + P2_W + POOL_W]
    pz = proj[:, P1_W + P2_W + POOL_W:P1_W + P2_W + P3_W]
    fg = proj[:, P1_W + P2_W + P3_W:]

    row = i * tm + lax.broadcasted_iota(jnp.int32, (tm, LANE), 0)
    logf = jnp.where(row >= FRONT, jax.nn.log_sigmoid(fg + bf_ref[...]), 0.0)
    tri = (lax.broadcasted_iota(jnp.int32, (LANE, LANE), 0)
           >= lax.broadcasted_iota(jnp.int32, (LANE, LANE), 1)).astype(BF16)
    carry = carry_ref[0:1, :]
    for s in range(tm // LANE):
        parts = _split3(logf[s * LANE:(s + 1) * LANE])
        within = sum(jnp.dot(tri, p, preferred_element_type=F32) for p in parts)
        cblk = within + carry
        cf_ref[0, s * LANE:(s + 1) * LANE, :] = cblk
        carry = cblk[LANE - 1:LANE, :]
    carry_ref[...] = jnp.broadcast_to(carry, carry_ref.shape)

    u_ref[halo:halo + tm, :] = pu
    acc = pu
    sums = {}
    for k in range(1, halo):
        acc = acc + u_ref[halo - k:halo - k + tm, :]
        if k + 1 in POOL_WINDOWS:
            sums[k + 1] = acc
    prow = i * tm + lax.broadcasted_iota(jnp.int32, (tm, POOL_W), 0)
    plane = lax.broadcasted_iota(jnp.int32, (tm, POOL_W), 1)
    pos1 = prow - (FRONT - 1)
    gd = POOL_W // len(POOL_WINDOWS)
    mean = None
    for gi, wdw in reversed(list(enumerate(POOL_WINDOWS))):
        cand = sums[wdw] / jnp.clip(pos1, 1, wdw).astype(F32)
        mean = cand if mean is None else jnp.where(plane < (gi + 1) * gd, cand, mean)
    pooled = mean - pu
    y = jnp.dot(pooled.astype(BF16), wp_ref[...], preferred_element_type=F32)
    cp_ref[0] = (y * ps_ref[...] * jax.nn.silu(pz)).astype(BF16)
    u_ref[0:halo, :] = pu[tm - halo:tm, :]


def _proj_call(h, g, w, bfp, wp, ps, tm):
    B, Lq, D = h.shape
    nt = Lq // tm
    row_spec = lambda width: pl.BlockSpec((1, tm, width), lambda b, i: (b, i, 0))
    full = lambda a: pl.BlockSpec(a.shape, lambda b, i: (0,) * a.ndim)
    return pl.pallas_call(
        _proj_kernel,
        grid=(B, nt),
        in_specs=[row_spec(D), full(g), full(w), full(bfp), full(wp), full(ps)],
        out_specs=[row_spec(P1_W), row_spec(P2_W), row_spec(FORGET_PAD), row_spec(POOL_W)],
        out_shape=[jax.ShapeDtypeStruct((B, Lq, P1_W), BF16),
                   jax.ShapeDtypeStruct((B, Lq, P2_W), BF16),
                   jax.ShapeDtypeStruct((B, Lq, FORGET_PAD), F32),
                   jax.ShapeDtypeStruct((B, Lq, POOL_W), BF16)],
        scratch_shapes=[pltpu.VMEM((8, LANE), F32),
                        pltpu.VMEM((POOL_WINDOWS[-1] + tm, POOL_W), F32)],
        compiler_params=pltpu.CompilerParams(
            dimension_semantics=("arbitrary", "arbitrary"), vmem_limit_bytes=VMEM_LIMIT),
        name="proj",
    )(h, g, w, bfp, wp, ps)


def _attn_kernel(*refs, diff, lambda_init):
    if diff:
        (q_ref, k_ref, v_ref, z_ref, lq1_ref, lk1_ref, lq2_ref, lk2_ref, sg_ref,
         o_ref, qa_ref, ka_ref, vt_ref, ot_ref, dm_ref) = refs
    else:
        (q_ref, k_ref, v_ref, z_ref, c_ref,
         o_ref, qa_ref, ka_ref, vt_ref, ot_ref, dm_ref) = refs
    lq = q_ref.shape[1]
    n_chunks = lq // LANE
    n_qblocks = (lq - LANE) // TQ
    pid = pl.program_id(1)
    dv = LANE if diff else HEAD

    if diff:
        slope = jnp.where(pid == 0, 2.0 ** -2,
                          jnp.where(pid == 1, 2.0 ** -4,
                                    jnp.where(pid == 2, 2.0 ** -6, 2.0 ** -8))).astype(F32)

    kk = lax.broadcasted_iota(jnp.int32, (TQ, TQ), 0)
    qq = lax.broadcasted_iota(jnp.int32, (TQ, TQ), 1)
    if diff:
        fix = jnp.where(kk > qq, (2.0 * slope) * (qq - kk).astype(F32), 0.0)
        dm_ref[...] = jnp.where(kk // CHUNK > qq // CHUNK, MASK, fix)
    else:
        dm_ref[...] = jnp.where(kk > qq, MASK, 0.0)

    lane = lax.broadcasted_iota(jnp.int32, (LANE, LANE), 1)
    rloc = lax.broadcasted_iota(jnp.int32, (LANE, 1), 0)

    def build(r, _):
        rows = pl.ds(pl.multiple_of(r * LANE, LANE), LANE)
        q = q_ref[0, rows, :].astype(F32) * (HEAD ** -0.5)
        k = k_ref[0, rows, :].astype(F32)
        grow = r * LANE + rloc
        dead = jnp.where(grow < FRONT, MASK, 0.0)
        if not diff:
            c = c_ref[0, rows, :]
        for inst in range(2):
            if diff:
                qb = -slope * grow.astype(F32)
            else:
                qb = jnp.where(pid == 0, c[:, inst:inst + 1], c[:, 2 + inst:3 + inst])
            kb = -qb
            base = HEAD * (1 - inst)
            own = (lane >= HEAD * inst) & (lane < HEAD * (inst + 1))
            qparts = _split3(qb)
            kparts = _split3(kb)
            qx = jnp.zeros((LANE, LANE), F32)
            kx = jnp.zeros((LANE, LANE), F32)
            for j in range(3):
                qx = jnp.where(lane == base + j, qparts[j].astype(F32), qx)
                kx = jnp.where(lane == base + j, 1.0, kx)
                qx = jnp.where(lane == base + 3 + j, 1.0, qx)
                kx = jnp.where(lane == base + 3 + j, kparts[j].astype(F32), kx)
            qx = jnp.where(lane == base + 6, 1.0, qx)
            kx = jnp.where(lane == base + 6, dead, kx)
            qa_ref[inst, rows, :] = jnp.where(own, q, qx).astype(BF16)
            ka_ref[inst, rows, :] = jnp.where(own, k, kx).astype(BF16)
        vt_ref[r] = v_ref[0, rows, :].astype(F32).T.astype(BF16)
        return 0

    lax.fori_loop(0, n_chunks, build, 0)

    if diff:
        lam = (jnp.exp(jnp.sum(lq1_ref[...] * lk1_ref[...], axis=-1, keepdims=True))
               - jnp.exp(jnp.sum(lq2_ref[...] * lk2_ref[...], axis=-1, keepdims=True))
               + lambda_init)

    def vrows(inst):
        return slice(0, LANE) if diff else slice(HEAD * inst, HEAD * (inst + 1))

    def step(states, qas, key_start, n_key_chunks, first_chunk, bias):
        tk = n_key_chunks * LANE
        out = []
        for inst in range(2):
            m, l, acc = states[inst]
            ka = ka_ref[inst, pl.ds(key_start, tk), :]
            s = lax.dot_general(ka, qas[inst], (((1,), (1,)), ((), ())), preferred_element_type=F32)
            if bias is not None:
                s = s + bias
            m_new = jnp.maximum(m, jnp.max(s, axis=0, keepdims=True))
            alpha = jnp.exp(m - m_new)
            p = jnp.exp(s - m_new)
            l = alpha * l + jnp.sum(p, axis=0, keepdims=True)
            pb = p.astype(BF16)
            pv = None
            for cch in range(n_key_chunks):
                vt = vt_ref[first_chunk + cch, vrows(inst), :]
                d = jnp.dot(vt, pb[cch * LANE:(cch + 1) * LANE], preferred_element_type=F32)
                pv = d if pv is None else pv + d
            out.append((m_new, l, alpha * acc + pv))
        return out

    def init_states(tq):
        return [(jnp.full((1, tq), M_INIT, F32), jnp.zeros((1, tq), F32), jnp.zeros((dv, tq), F32))
                for _ in range(2)]

    def finish(states, first_chunk, n_q_chunks):
        if diff:
            (_, l0, a0), (_, l1, a1) = states
            o = a0 / l0 - lam * (a1 / l1)
            o = o * lax.rsqrt(jnp.mean(o * o, axis=0, keepdims=True) + SUBLN_EPS) * sg_ref[...]
            o = o * (1.0 - lambda_init)
            for cch in range(n_q_chunks):
                ot_ref[first_chunk + cch] = o[:, cch * LANE:(cch + 1) * LANE]
        else:
            o = jnp.concatenate([acc / l for (_, l, acc) in states], axis=0)
            for cch in range(n_q_chunks):
                ot_ref[first_chunk + cch] = o[:, cch * LANE:(cch + 1) * LANE]

    qas0 = [qa_ref[inst, 0:LANE, :] for inst in range(2)]
    st0 = step(init_states(LANE), qas0, 0, 1, 0, dm_ref[0:LANE, 0:LANE])
    finish(st0, 0, 1)

    kc = TQ // LANE

    def qblock(i, _):
        q_start = pl.multiple_of(LANE + i * TQ, LANE)
        qas = [qa_ref[inst, pl.ds(q_start, TQ), :] for inst in range(2)]
        states = step(init_states(TQ), qas, 0, 1, 0, None)

        def kvblock(j, carry):
            st = [(carry[0], carry[1], carry[2]), (carry[3], carry[4], carry[5])]
            k_start = pl.multiple_of(LANE + j * TQ, LANE)
            st = step(st, qas, k_start, kc, 1 + kc * j, None)
            return (*st[0], *st[1])

        carry = lax.fori_loop(0, i, kvblock, (*states[0], *states[1]))
        states = [(carry[0], carry[1], carry[2]), (carry[3], carry[4], carry[5])]
        states = step(states, qas, q_start, kc, 1 + kc * i, dm_ref[...])
        finish(states, 1 + kc * i, kc)
        return 0

    lax.fori_loop(0, n_qblocks, qblock, 0)

    def emit(r, _):
        rows = pl.ds(pl.multiple_of(r * LANE, LANE), LANE)
        gate = jax.nn.silu(z_ref[0, rows, :].astype(F32))
        o_ref[0, rows, :] = (ot_ref[r].T * gate).astype(BF16)
        return 0

    lax.fori_loop(0, n_chunks, emit, 0)


def _attn_call(p, extra, *, diff, lambda_init, col_blocks):
    B, Lq, _ = p.shape
    n_groups = col_blocks
    slab = lambda which: pl.BlockSpec((1, Lq, LANE), lambda b, g, which=which: (b, 0, which * n_groups + g))
    in_specs = [slab(0), slab(1), slab(2), slab(3)]
    args = [p, p, p, p]
    if diff:
        for a in extra:
            in_specs.append(pl.BlockSpec(a.shape, lambda b, g, nd=a.ndim: (0,) * nd))
            args.append(a)
    else:
        (cf,) = extra
        in_specs.append(pl.BlockSpec((1, Lq, FORGET_PAD), lambda b, g: (b, 0, 0)))
        args.append(cf)
    n_chunks = Lq // LANE
    return pl.pallas_call(
        functools.partial(_attn_kernel, diff=diff, lambda_init=lambda_init),
        grid=(B, n_groups),
        in_specs=in_specs,
        out_specs=pl.BlockSpec((1, Lq, LANE), lambda b, g: (b, 0, g)),
        out_shape=jax.ShapeDtypeStruct((B, Lq, n_groups * LANE), BF16),
        scratch_shapes=[pltpu.VMEM((2, Lq, LANE), BF16),
                        pltpu.VMEM((2, Lq, LANE), BF16),
                        pltpu.VMEM((n_chunks, LANE, LANE), BF16),
                        pltpu.VMEM((n_chunks, LANE, LANE), F32),
                        pltpu.VMEM((TQ, TQ), F32)],
        compiler_params=pltpu.CompilerParams(
            dimension_semantics=("arbitrary", "arbitrary"), vmem_limit_bytes=VMEM_LIMIT),
        name="diff_attn" if diff else "fox_attn",
    )(*args)


def _out_kernel(h_ref, a_ref, b_ref, c_ref, w_ref, *rest, final):
    if final:
        g_ref, o_ref = rest
    else:
        (o_ref,) = rest
    hn = (h_ref[0]
          + jnp.dot(a_ref[0], w_ref[0:FOX_W, :], preferred_element_type=F32)
          + jnp.dot(b_ref[0], w_ref[FOX_W:FOX_W + DIFF_W, :], preferred_element_type=F32)
          + jnp.dot(c_ref[0], w_ref[FOX_W + DIFF_W:, :], preferred_element_type=F32))
    if final:
        hn = hn * lax.rsqrt(jnp.mean(hn * hn, axis=-1, keepdims=True) + NORM_EPS) * g_ref[...]
    o_ref[0] = hn


def _out_call(h, a, b, c, w, tm):
    B, Lq, D = h.shape
    row_spec = lambda width: pl.BlockSpec((1, tm, width), lambda bb, i: (bb, i, 0))
    return pl.pallas_call(
        functools.partial(_out_kernel, final=False),
        grid=(B, Lq // tm),
        in_specs=[row_spec(D), row_spec(FOX_W), row_spec(DIFF_W), row_spec(POOL_W),
                  pl.BlockSpec(w.shape, lambda bb, i: (0, 0))],
        out_specs=row_spec(D),
        out_shape=jax.ShapeDtypeStruct((B, Lq, D), F32),
        compiler_params=pltpu.CompilerParams(
            dimension_semantics=("arbitrary", "arbitrary"), vmem_limit_bytes=VMEM_LIMIT),
        name="out_proj",
    )(h, a, b, c, w)


def _final_call(h, a, b, c, w, g):
    B, Lq, D = h.shape
    S = Lq - LANE
    row_spec = lambda width: pl.BlockSpec((1, LANE, width), lambda bb, i: (bb, i + 1, 0))
    return pl.pallas_call(
        functools.partial(_out_kernel, final=True),
        grid=(B, S // LANE),
        in_specs=[row_spec(D), row_spec(FOX_W), row_spec(DIFF_W), row_spec(POOL_W),
                  pl.BlockSpec(w.shape, lambda bb, i: (0, 0)),
                  pl.BlockSpec(g.shape, lambda bb, i: (0, 0))],
        out_specs=pl.BlockSpec((1, LANE, D), lambda bb, i: (bb, i, 0)),
        out_shape=jax.ShapeDtypeStruct((B, S, D), F32),
        compiler_params=pltpu.CompilerParams(
            dimension_semantics=("arbitrary", "arbitrary"), vmem_limit_bytes=VMEM_LIMIT),
        name="out_proj_final",
    )(h, a, b, c, w, g)


def _row_tile(lq):
    for tm in (384, 256, 128):
        if lq % tm == 0:
            return tm
    raise ValueError(f"sequence of {lq} rows is not a multiple of {LANE}")


def _arrange_w_in(w):
    o = _IN_OFF
    pad = jnp.zeros((w.shape[0], FORGET_PAD - FOX_HEADS), w.dtype)
    return jnp.concatenate([w[:, o[0]:o[4]], w[:, o[5]:o[9]], w[:, o[9]:o[11]], w[:, o[4]:o[5]], pad],
                           axis=1).astype(BF16)


def kernel(x, meta_tokens, norm_g, w_in, b_f, lam_q1, lam_k1, lam_q2, lam_k2, subln_g, w_pool,
           pool_scale, w_out, final_g):
    B, S, D = x.shape
    depth = w_in.shape[0]
    assert S % TQ == 0 and D == P1_W
    Lq = LANE + S
    tm = _row_tile(Lq)
    meta = jnp.broadcast_to(meta_tokens.astype(x.dtype)[None], (B, N_META, D))
    h = jnp.concatenate([jnp.zeros((B, FRONT, D), x.dtype), meta, x], axis=1)

    y = None
    for l in range(depth):
        lambda_init = 0.8 - 0.6 * math.exp(-0.3 * l)
        w = _arrange_w_in(w_in[l])
        bfp = jnp.pad(b_f[l].astype(F32), (0, FORGET_PAD - FOX_HEADS))[None]
        wp = jax.scipy.linalg.block_diag(*[w_pool[l, g] for g in range(len(POOL_WINDOWS))]).astype(BF16)
        p1, p2, cf, cp = _proj_call(h, norm_g[l][None].astype(F32), w, bfp, wp,
                                    pool_scale[l][None].astype(F32), tm)
        a = _attn_call(p1, (cf,), diff=False, lambda_init=lambda_init, col_blocks=FOX_W // LANE)
        extra = (lam_q1[l][None].astype(F32), lam_k1[l][None].astype(F32),
                 lam_q2[l][None].astype(F32), lam_k2[l][None].astype(F32),
                 subln_g[l][:, None].astype(F32))
        bo = _attn_call(p2, extra, diff=True, lambda_init=lambda_init, col_blocks=DIFF_W // LANE)
        wo = w_out[l].astype(BF16)
        if l + 1 < depth:
            h = _out_call(h, a, bo, cp, wo, tm)
        else:
            y = _final_call(h, a, bo, cp, wo, final_g[None].astype(F32))
    return y
```

```python
import functools
import math

import jax
import jax.numpy as jnp
from jax import lax
from jax.experimental import pallas as pl
from jax.experimental.pallas import tpu as pltpu

F32 = jnp.float32
BF16 = jnp.bfloat16

LANE = 128
N_META = 16
CHUNK = 64
FRONT = LANE - N_META
HEAD = 64
FOX_HEADS = 4
DIFF_HEADS = 4
FOX_W = 256
DIFF_W = 512
POOL_W = 256
POOL_WINDOWS = (2, 4, 8, 16)
NORM_EPS = 1e-6
SUBLN_EPS = 1e-5
MASK = -1e30
M_INIT = -1e38
FORGET_PAD = LANE
TQ = 256
QSUB = 2
LOG2E = 1.4426950408889634
VMEM_LIMIT = 56 * 1024 * 1024

_IN_SIZES = (FOX_W, FOX_W, FOX_W, FOX_W, FOX_HEADS, DIFF_W, DIFF_W, DIFF_W, DIFF_W, POOL_W, POOL_W)
_IN_OFF = tuple(int(sum(_IN_SIZES[:i])) for i in range(len(_IN_SIZES) + 1))
P1_W = 4 * FOX_W
P2_W = 4 * DIFF_W
P3_W = 2 * POOL_W
W_COLS = P1_W + P2_W + P3_W + FORGET_PAD


def _split3(x):
    hi = x.astype(BF16)
    r1 = x - hi.astype(F32)
    mid = r1.astype(BF16)
    lo = (r1 - mid.astype(F32)).astype(BF16)
    return hi, mid, lo


def _proj_kernel(h_ref, g_ref, w_ref, bf_ref, wp_ref, ps_ref,
                 p1_ref, p2_ref, cf_ref, cp_ref, carry_ref, u_ref):
    i = pl.program_id(1)
    tm = h_ref.shape[1]
    halo = POOL_WINDOWS[-1]

    @pl.when(i == 0)
    def _():
        carry_ref[...] = jnp.zeros_like(carry_ref)
        u_ref[0:halo, :] = jnp.zeros((halo, POOL_W), F32)

    h = h_ref[0]
    hn = h * lax.rsqrt(jnp.mean(h * h, axis=-1, keepdims=True) + NORM_EPS) * g_ref[...]
    proj = jnp.dot(hn.astype(BF16), w_ref[...], preferred_element_type=F32)
    p1_ref[0] = proj[:, 0:P1_W].astype(BF16)
    p2_ref[0] = proj[:, P1_W:P1_W + P2_W].astype(BF16)
    pu = proj[:, P1_W + P2_W:P1_W + P2_W + POOL_W]
    pz = proj[:, P1_W + P2_W + POOL_W:P1_W + P2_W + P3_W]
    fg = proj[:, P1_W + P2_W + P3_W:]

    row = i * tm + lax.broadcasted_iota(jnp.int32, (tm, LANE), 0)
    logf = jnp.where(row >= FRONT, jax.nn.log_sigmoid(fg + bf_ref[...]), 0.0)
    tri = (lax.broadcasted_iota(jnp.int32, (LANE, LANE), 0)
           >= lax.broadcasted_iota(jnp.int32, (LANE, LANE), 1)).astype(BF16)
    carry = carry_ref[0:1, :]
    for s in range(tm // LANE):
        parts = _split3(logf[s * LANE:(s + 1) * LANE])
        within = sum(jnp.dot(tri, p, preferred_element_type=F32) for p in parts)
        cblk = within + carry
        cf_ref[0, s * LANE:(s + 1) * LANE, :] = cblk
        carry = cblk[LANE - 1:LANE, :]
    carry_ref[...] = jnp.broadcast_to(carry, carry_ref.shape)

    u_ref[halo:halo + tm, :] = pu
    acc = pu
    sums = {}
    for k in range(1, halo):
        acc = acc + u_ref[halo - k:halo - k + tm, :]
        if k + 1 in POOL_WINDOWS:
            sums[k + 1] = acc
    prow = i * tm + lax.broadcasted_iota(jnp.int32, (tm, POOL_W), 0)
    plane = lax.broadcasted_iota(jnp.int32, (tm, POOL_W), 1)
    pos1 = prow - (FRONT - 1)
    gd = POOL_W // len(POOL_WINDOWS)
    mean = None
    for gi, wdw in reversed(list(enumerate(POOL_WINDOWS))):
        cand = sums[wdw] / jnp.clip(pos1, 1, wdw).astype(F32)
        mean = cand if mean is None else jnp.where(plane < (gi + 1) * gd, cand, mean)
    pooled = mean - pu
    y = jnp.dot(pooled.astype(BF16), wp_ref[...], preferred_element_type=F32)
    cp_ref[0] = (y * ps_ref[...] * jax.nn.silu(pz)).astype(BF16)
    u_ref[0:halo, :] = pu[tm - halo:tm, :]


def _proj_call(h, g, w, bfp, wp, ps, tm):
    B, Lq, D = h.shape
    nt = Lq // tm
    row_spec = lambda width: pl.BlockSpec((1, tm, width), lambda b, i: (b, i, 0))
    full = lambda a: pl.BlockSpec(a.shape, lambda b, i: (0,) * a.ndim)
    return pl.pallas_call(
        _proj_kernel,
        grid=(B, nt),
        in_specs=[row_spec(D), full(g), full(w), full(bfp), full(wp), full(ps)],
        out_specs=[row_spec(P1_W), row_spec(P2_W), row_spec(FORGET_PAD), row_spec(POOL_W)],
        out_shape=[jax.ShapeDtypeStruct((B, Lq, P1_W), BF16),
                   jax.ShapeDtypeStruct((B, Lq, P2_W), BF16),
                   jax.ShapeDtypeStruct((B, Lq, FORGET_PAD), F32),
                   jax.ShapeDtypeStruct((B, Lq, POOL_W), BF16)],
        scratch_shapes=[pltpu.VMEM((8, LANE), F32),
                        pltpu.VMEM((POOL_WINDOWS[-1] + tm, POOL_W), F32)],
        compiler_params=pltpu.CompilerParams(
            dimension_semantics=("arbitrary", "arbitrary"), vmem_limit_bytes=VMEM_LIMIT),
        name="proj",
    )(h, g, w, bfp, wp, ps)


def _attn_kernel(*refs, diff, lambda_init):
    if diff:
        (q_ref, k_ref, v_ref, z_ref, lq1_ref, lk1_ref, lq2_ref, lk2_ref, sg_ref,
         o_ref, qa_ref, ka_ref, vt_ref, ot_ref, dm_ref) = refs
    else:
        (q_ref, k_ref, v_ref, z_ref, c_ref,
         o_ref, qa_ref, ka_ref, vt_ref, ot_ref, dm_ref) = refs
    lq = q_ref.shape[1]
    n_chunks = lq // LANE
    n_sblocks = (lq - LANE) // (QSUB * TQ)
    pid = pl.program_id(1)
    dv = LANE if diff else HEAD

    if diff:
        slope = jnp.where(pid == 0, 2.0 ** -2,
                          jnp.where(pid == 1, 2.0 ** -4,
                                    jnp.where(pid == 2, 2.0 ** -6, 2.0 ** -8))).astype(F32)

    kk = lax.broadcasted_iota(jnp.int32, (TQ, TQ), 0)
    qq = lax.broadcasted_iota(jnp.int32, (TQ, TQ), 1)
    if diff:
        fix = jnp.where(kk > qq, (2.0 * LOG2E * slope) * (qq - kk).astype(F32), 0.0)
        dm_ref[...] = jnp.where(kk // CHUNK > qq // CHUNK, MASK, fix)
    else:
        dm_ref[...] = jnp.where(kk > qq, MASK, 0.0)

    lane = lax.broadcasted_iota(jnp.int32, (LANE, LANE), 1)
    rloc = lax.broadcasted_iota(jnp.int32, (LANE, 1), 0)

    def build(r, _):
        rows = pl.ds(pl.multiple_of(r * LANE, LANE), LANE)
        q = q_ref[0, rows, :].astype(F32) * (HEAD ** -0.5 * LOG2E)
        k = k_ref[0, rows, :].astype(F32)
        grow = r * LANE + rloc
        dead = jnp.where(grow < FRONT, MASK, 0.0)
        if not diff:
            c = c_ref[0, rows, :]
        for inst in range(2):
            if diff:
                qb = (-LOG2E * slope) * grow.astype(F32)
            else:
                qb = LOG2E * jnp.where(pid == 0, c[:, inst:inst + 1], c[:, 2 + inst:3 + inst])
            kb = -qb
            base = HEAD * (1 - inst)
            own = (lane >= HEAD * inst) & (lane < HEAD * (inst + 1))
            qparts = _split3(qb)
            kparts = _split3(kb)
            qx = jnp.zeros((LANE, LANE), F32)
            kx = jnp.zeros((LANE, LANE), F32)
            for j in range(3):
                qx = jnp.where(lane == base + j, qparts[j].astype(F32), qx)
                kx = jnp.where(lane == base + j, 1.0, kx)
                qx = jnp.where(lane == base + 3 + j, 1.0, qx)
                kx = jnp.where(lane == base + 3 + j, kparts[j].astype(F32), kx)
            qx = jnp.where(lane == base + 6, 1.0, qx)
            kx = jnp.where(lane == base + 6, dead, kx)
            qa_ref[inst, rows, :] = jnp.where(own, q, qx).astype(BF16)
            ka_ref[inst, rows, :] = jnp.where(own, k, kx).astype(BF16)
        vt_ref[r] = v_ref[0, rows, :].astype(F32).T.astype(BF16)
        return 0

    lax.fori_loop(0, n_chunks, build, 0)

    if diff:
        lam = (jnp.exp(jnp.sum(lq1_ref[...] * lk1_ref[...], axis=-1, keepdims=True))
               - jnp.exp(jnp.sum(lq2_ref[...] * lk2_ref[...], axis=-1, keepdims=True))
               + lambda_init)

    def vrows(inst):
        return slice(0, LANE) if diff else slice(HEAD * inst, HEAD * (inst + 1))

    def partial(inst, qa, key_start, n_key_chunks, first_chunk, bias):
        ka = ka_ref[inst, pl.ds(key_start, n_key_chunks * LANE), :]
        s = lax.dot_general(ka, qa, (((1,), (1,)), ((), ())), preferred_element_type=F32)
        if bias is not None:
            s = s + bias
        m = jnp.max(s, axis=0, keepdims=True)
        p = jnp.exp2(s - m)
        l = jnp.sum(p, axis=0, keepdims=True)
        pb = p.astype(BF16)
        pv = None
        for cch in range(n_key_chunks):
            vt = vt_ref[first_chunk + cch, vrows(inst), :]
            d = jnp.dot(vt, pb[cch * LANE:(cch + 1) * LANE], preferred_element_type=F32)
            pv = d if pv is None else pv + d
        return m, l, pv

    def merge(state, parts):
        m, l, acc = state
        m_new = m
        for mp, _, _ in parts:
            m_new = jnp.maximum(m_new, mp)
        w = jnp.exp2(m - m_new)
        l = w * l
        acc = w * acc
        for mp, lp, pvp in parts:
            wp = jnp.exp2(mp - m_new)
            l = l + wp * lp
            acc = acc + wp * pvp
        return m_new, l, acc

    def init_state(tq):
        return (jnp.full((1, tq), M_INIT, F32), jnp.zeros((1, tq), F32), jnp.zeros((dv, tq), F32))

    def finish(states, first_chunk, n_q_chunks):
        if diff:
            (_, l0, a0), (_, l1, a1) = states
            o = a0 / l0 - lam * (a1 / l1)
            o = o * lax.rsqrt(jnp.mean(o * o, axis=0, keepdims=True) + SUBLN_EPS) * sg_ref[...]
            o = o * (1.0 - lambda_init)
        else:
            o = jnp.concatenate([acc / l for (_, l, acc) in states], axis=0)
        for cch in range(n_q_chunks):
            ot_ref[first_chunk + cch] = o[:, cch * LANE:(cch + 1) * LANE]

    lead = [merge(init_state(LANE),
                  [partial(inst, qa_ref[inst, 0:LANE, :], 0, 1, 0, dm_ref[0:LANE, 0:LANE])])
            for inst in range(2)]
    finish(lead, 0, 1)

    kc = TQ // LANE
    pairs = [(inst, qs) for inst in range(2) for qs in range(QSUB)]

    def superblock(i, _):
        q_start = pl.multiple_of(LANE + i * (QSUB * TQ), LANE)
        qas = {(inst, qs): qa_ref[inst, pl.ds(q_start + qs * TQ, TQ), :] for inst, qs in pairs}

        def kvstep(j, carry):
            out = []
            for n, (inst, qs) in enumerate(pairs):
                parts = []
                for u in range(QSUB):
                    blk = QSUB * j + u
                    k_start = pl.multiple_of(LANE + blk * TQ, LANE)
                    parts.append(partial(inst, qas[inst, qs], k_start, kc, 1 + kc * blk, None))
                out.extend(merge(carry[3 * n:3 * n + 3], parts))
            return tuple(out)

        carry0 = tuple(x for _ in pairs for x in init_state(TQ))
        carry = lax.fori_loop(0, i, kvstep, carry0)

        for qs in range(QSUB):
            states = []
            for inst in range(2):
                n = pairs.index((inst, qs))
                parts = [partial(inst, qas[inst, qs], 0, 1, 0, None)]
                for u in range(qs + 1):
                    blk = QSUB * i + u
                    k_start = pl.multiple_of(LANE + blk * TQ, LANE)
                    parts.append(partial(inst, qas[inst, qs], k_start, kc, 1 + kc * blk,
                                         dm_ref[...] if u == qs else None))
                states.append(merge(carry[3 * n:3 * n + 3], parts))
            finish(states, 1 + kc * (QSUB * i + qs), kc)
        return 0

    lax.fori_loop(0, n_sblocks, superblock, 0)

    def emit(r, _):
        rows = pl.ds(pl.multiple_of(r * LANE, LANE), LANE)
        gate = jax.nn.silu(z_ref[0, rows, :].astype(F32))
        o_ref[0, rows, :] = (ot_ref[r].T * gate).astype(BF16)
        return 0

    lax.fori_loop(0, n_chunks, emit, 0)


def _attn_call(p, extra, *, diff, lambda_init, col_blocks):
    B, Lq, _ = p.shape
    n_groups = col_blocks
    slab = lambda which: pl.BlockSpec((1, Lq, LANE), lambda b, g, which=which: (b, 0, which * n_groups + g))
    in_specs = [slab(0), slab(1), slab(2), slab(3)]
    args = [p, p, p, p]
    if diff:
        for a in extra:
            in_specs.append(pl.BlockSpec(a.shape, lambda b, g, nd=a.ndim: (0,) * nd))
            args.append(a)
    else:
        (cf,) = extra
        in_specs.append(pl.BlockSpec((1, Lq, FORGET_PAD), lambda b, g: (b, 0, 0)))
        args.append(cf)
    n_chunks = Lq // LANE
    return pl.pallas_call(
        functools.partial(_attn_kernel, diff=diff, lambda_init=lambda_init),
        grid=(B, n_groups),
        in_specs=in_specs,
        out_specs=pl.BlockSpec((1, Lq, LANE), lambda b, g: (b, 0, g)),
        out_shape=jax.ShapeDtypeStruct((B, Lq, n_groups * LANE), BF16),
        scratch_shapes=[pltpu.VMEM((2, Lq, LANE), BF16),
                        pltpu.VMEM((2, Lq, LANE), BF16),
                        pltpu.VMEM((n_chunks, LANE, LANE), BF16),
                        pltpu.VMEM((n_chunks, LANE, LANE), F32),
                        pltpu.VMEM((TQ, TQ), F32)],
        compiler_params=pltpu.CompilerParams(
            dimension_semantics=("arbitrary", "arbitrary"), vmem_limit_bytes=VMEM_LIMIT),
        name="diff_attn" if diff else "fox_attn",
    )(*args)


def _out_kernel(h_ref, a_ref, b_ref, c_ref, w_ref, *rest, final):
    if final:
        g_ref, o_ref = rest
    else:
        (o_ref,) = rest
    hn = (h_ref[0]
          + jnp.dot(a_ref[0], w_ref[0:FOX_W, :], preferred_element_type=F32)
          + jnp.dot(b_ref[0], w_ref[FOX_W:FOX_W + DIFF_W, :], preferred_element_type=F32)
          + jnp.dot(c_ref[0], w_ref[FOX_W + DIFF_W:, :], preferred_element_type=F32))
    if final:
        hn = hn * lax.rsqrt(jnp.mean(hn * hn, axis=-1, keepdims=True) + NORM_EPS) * g_ref[...]
    o_ref[0] = hn


def _out_call(h, a, b, c, w, tm):
    B, Lq, D = h.shape
    row_spec = lambda width: pl.BlockSpec((1, tm, width), lambda bb, i: (bb, i, 0))
    return pl.pallas_call(
        functools.partial(_out_kernel, final=False),
        grid=(B, Lq // tm),
        in_specs=[row_spec(D), row_spec(FOX_W), row_spec(DIFF_W), row_spec(POOL_W),
                  pl.BlockSpec(w.shape, lambda bb, i: (0, 0))],
        out_specs=row_spec(D),
        out_shape=jax.ShapeDtypeStruct((B, Lq, D), F32),
        compiler_params=pltpu.CompilerParams(
            dimension_semantics=("arbitrary", "arbitrary"), vmem_limit_bytes=VMEM_LIMIT),
        name="out_proj",
    )(h, a, b, c, w)


def _final_call(h, a, b, c, w, g):
    B, Lq, D = h.shape
    S = Lq - LANE
    row_spec = lambda width: pl.BlockSpec((1, LANE, width), lambda bb, i: (bb, i + 1, 0))
    return pl.pallas_call(
        functools.partial(_out_kernel, final=True),
        grid=(B, S // LANE),
        in_specs=[row_spec(D), row_spec(FOX_W), row_spec(DIFF_W), row_spec(POOL_W),
                  pl.BlockSpec(w.shape, lambda bb, i: (0, 0)),
                  pl.BlockSpec(g.shape, lambda bb, i: (0, 0))],
        out_specs=pl.BlockSpec((1, LANE, D), lambda bb, i: (bb, i, 0)),
        out_shape=jax.ShapeDtypeStruct((B, S, D), F32),
        compiler_params=pltpu.CompilerParams(
            dimension_semantics=("arbitrary", "arbitrary"), vmem_limit_bytes=VMEM_LIMIT),
        name="out_proj_final",
    )(h, a, b, c, w, g)


def _row_tile(lq):
    for tm in (384, 256, 128):
        if lq % tm == 0:
            return tm
    raise ValueError(f"sequence of {lq} rows is not a multiple of {LANE}")


def _arrange_w_in(w):
    o = _IN_OFF
    pad = jnp.zeros((w.shape[0], FORGET_PAD - FOX_HEADS), w.dtype)
    return jnp.concatenate([w[:, o[0]:o[4]], w[:, o[5]:o[9]], w[:, o[9]:o[11]], w[:, o[4]:o[5]], pad],
                           axis=1).astype(BF16)


def kernel(x, meta_tokens, norm_g, w_in, b_f, lam_q1, lam_k1, lam_q2, lam_k2, subln_g, w_pool,
           pool_scale, w_out, final_g):
    B, S, D = x.shape
    depth = w_in.shape[0]
    assert S % (QSUB * TQ) == 0 and D == P1_W
    Lq = LANE + S
    tm = _row_tile(Lq)
    meta = jnp.broadcast_to(meta_tokens.astype(x.dtype)[None], (B, N_META, D))
    h = jnp.concatenate([jnp.zeros((B, FRONT, D), x.dtype), meta, x], axis=1)

    y = None
    for l in range(depth):
        lambda_init = 0.8 - 0.6 * math.exp(-0.3 * l)
        w = _arrange_w_in(w_in[l])
        bfp = jnp.pad(b_f[l].astype(F32), (0, FORGET_PAD - FOX_HEADS))[None]
        wp = jax.scipy.linalg.block_diag(*[w_pool[l, g] for g in range(len(POOL_WINDOWS))]).astype(BF16)
        p1, p2, cf, cp = _proj_call(h, norm_g[l][None].astype(F32), w, bfp, wp,
                                    pool_scale[l][None].astype(F32), tm)
        a = _attn_call(p1, (cf,), diff=False, lambda_init=lambda_init, col_blocks=FOX_W // LANE)
        extra = (lam_q1[l][None].astype(F32), lam_k1[l][None].astype(F32),
                 lam_q2[l][None].astype(F32), lam_k2[l][None].astype(F32),
                 subln_g[l][:, None].astype(F32))
        bo = _attn_call(p2, extra, diff=True, lambda_init=lambda_init, col_blocks=DIFF_W // LANE)
        wo = w_out[l].astype(BF16)
        if l + 1 < depth:
            h = _out_call(h, a, bo, cp, wo, tm)
        else:
            y = _final_call(h, a, bo, cp, wo, final_g[None].astype(F32))
    return y
```

```python
import functools
import math

import jax
import jax.numpy as jnp
from jax import lax
from jax.experimental import pallas as pl
from jax.experimental.pallas import tpu as pltpu

F32 = jnp.float32
BF16 = jnp.bfloat16

LANE = 128
N_META = 16
CHUNK = 64
FRONT = LANE - N_META
HEAD = 64
FOX_HEADS = 4
DIFF_HEADS = 4
N_INST = FOX_HEADS + 2 * DIFF_HEADS
FOX_W = 256
DIFF_W = 512
ATT_W = FOX_W + DIFF_W
POOL_W = 256
POOL_WINDOWS = (2, 4, 8, 16)
ALIBI_SLOPES = tuple(2.0 ** (-8.0 * (h + 1) / DIFF_HEADS) for h in range(DIFF_HEADS))
NORM_EPS = 1e-6
SUBLN_EPS = 1e-5
MASK = -1e30
M_INIT = -1e38
TQ = 256
QSUB = 2
NI = 4
GROUP_W = NI * HEAD
LOG2E = 1.4426950408889634
VMEM_LIMIT = 56 * 1024 * 1024

HEAD_W = 4 * FOX_W
TAIL_W = 4 * DIFF_W + 2 * POOL_W
TAIL_PAD = TAIL_W + LANE
W_COLS = HEAD_W + TAIL_PAD
FORGET_LANE = LANE - FOX_HEADS
C_FQ, C_FK, C_FV, C_FZ = 0, FOX_W, 2 * FOX_W, 3 * FOX_W
C_DQ, C_DK, C_DV, C_DZ = HEAD_W, HEAD_W + DIFF_W, HEAD_W + 2 * DIFF_W, HEAD_W + 3 * DIFF_W
C_PU, C_PZ, C_FG = HEAD_W + 4 * DIFF_W, HEAD_W + 4 * DIFF_W + POOL_W, HEAD_W + TAIL_W


def _split3(x):
    hi = x.astype(BF16)
    r1 = x - hi.astype(F32)
    mid = r1.astype(BF16)
    lo = (r1 - mid.astype(F32)).astype(BF16)
    return hi, mid, lo


def _arrange_kernel(head_ref, tail_ref, o_ref):
    o_ref[0, :, 0:HEAD_W] = head_ref[0].astype(BF16)
    o_ref[0, :, HEAD_W:] = pltpu.roll(tail_ref[0], TAIL_PAD - FOX_HEADS, axis=1).astype(BF16)


def _arrange_call(w_in):
    depth, d, n = w_in.shape
    head = w_in[:, :, 0:HEAD_W]
    tail = jnp.pad(w_in[:, :, HEAD_W:], ((0, 0), (0, 0), (0, TAIL_PAD - (n - HEAD_W))))
    rows = 256
    return pl.pallas_call(
        _arrange_kernel,
        grid=(depth, d // rows),
        in_specs=[pl.BlockSpec((1, rows, HEAD_W), lambda l, i: (l, i, 0)),
                  pl.BlockSpec((1, rows, TAIL_PAD), lambda l, i: (l, i, 0))],
        out_specs=pl.BlockSpec((1, rows, W_COLS), lambda l, i: (l, i, 0)),
        out_shape=jax.ShapeDtypeStruct((depth, d, W_COLS), BF16),
        compiler_params=pltpu.CompilerParams(
            dimension_semantics=("arbitrary", "arbitrary"), vmem_limit_bytes=VMEM_LIMIT),
        name="arrange_w",
    )(head, tail)


def _proj_kernel(h_ref, g_ref, w_ref, bf_ref, wp_ref, ps_ref,
                 ka_ref, qat_ref, vt_ref, z_ref, cp_ref, carry_ref, u_ref):
    i = pl.program_id(1)
    tm = h_ref.shape[1]
    halo = POOL_WINDOWS[-1]

    @pl.when(i == 0)
    def _():
        carry_ref[...] = jnp.zeros_like(carry_ref)
        u_ref[0:halo, :] = jnp.zeros((halo, POOL_W), F32)

    h = h_ref[0]
    hn = h * lax.rsqrt(jnp.mean(h * h, axis=-1, keepdims=True) + NORM_EPS) * g_ref[...]
    proj = jnp.dot(hn.astype(BF16), w_ref[0], preferred_element_type=F32)
    z_ref[0, :, 0:FOX_W] = proj[:, C_FZ:C_FZ + FOX_W].astype(BF16)
    z_ref[0, :, FOX_W:] = proj[:, C_DZ:C_DZ + DIFF_W].astype(BF16)
    pu = proj[:, C_PU:C_PU + POOL_W]
    pz = proj[:, C_PZ:C_PZ + POOL_W]
    fg = proj[:, C_FG:]

    row = i * tm + lax.broadcasted_iota(jnp.int32, (tm, LANE), 0)
    logf = jnp.where(row >= FRONT, jax.nn.log_sigmoid(fg + bf_ref[...]), 0.0)
    tri = (lax.broadcasted_iota(jnp.int32, (LANE, LANE), 0)
           >= lax.broadcasted_iota(jnp.int32, (LANE, LANE), 1)).astype(BF16)
    lane = lax.broadcasted_iota(jnp.int32, (LANE, LANE), 1)
    lm = lane & (HEAD - 1)
    low_half = lane < HEAD
    rloc = lax.broadcasted_iota(jnp.int32, (LANE, 1), 0)

    def extras(q_bias, k_bias, dead):
        qh, qm, ql = _split3(q_bias)
        kh, km, kl = _split3(k_bias)
        one_q = jnp.where((lm >= 3) & (lm < 7), 1.0, 0.0)
        eq = jnp.where(lm == 0, qh.astype(F32), jnp.where(lm == 1, qm.astype(F32),
                                                          jnp.where(lm == 2, ql.astype(F32), one_q)))
        one_k = jnp.where(lm < 3, 1.0, 0.0)
        ek = jnp.where(lm == 3, kh.astype(F32), jnp.where(lm == 4, km.astype(F32),
                                                          jnp.where(lm == 5, kl.astype(F32),
                                                                    jnp.where(lm == 6, dead, one_k))))
        return eq, ek

    carry = carry_ref[0:1, :]
    for s in range(tm // LANE):
        rs = slice(s * LANE, (s + 1) * LANE)
        parts = _split3(logf[rs])
        within = sum(jnp.dot(tri, p, preferred_element_type=F32) for p in parts)
        cblk = within + carry
        carry = cblk[LANE - 1:LANE, :]

        grow = i * tm + s * LANE + rloc
        dead = jnp.where(grow < FRONT, MASK, 0.0)
        for hd in range(FOX_HEADS):
            cb = LOG2E * cblk[:, FORGET_LANE + hd:FORGET_LANE + hd + 1]
            eq, ek = extras(cb, -cb, dead)
            grp = slice((hd // 2) * LANE, (hd // 2 + 1) * LANE)
            own = low_half if hd % 2 == 0 else jnp.logical_not(low_half)
            q = proj[rs, C_FQ:C_FQ + FOX_W][:, grp] * (HEAD ** -0.5 * LOG2E)
            k = proj[rs, C_FK:C_FK + FOX_W][:, grp]
            ka_ref[0, hd, rs, :] = jnp.where(own, k, ek).astype(BF16)
            qat_ref[0, hd, s] = jnp.where(own, q, eq).T.astype(BF16)
        for hd in range(DIFF_HEADS):
            tb = (-LOG2E * ALIBI_SLOPES[hd]) * grow.astype(F32)
            eq, ek = extras(tb, -tb, dead)
            grp = slice(hd * LANE, (hd + 1) * LANE)
            q = proj[rs, C_DQ:C_DQ + DIFF_W][:, grp] * (HEAD ** -0.5 * LOG2E)
            k = proj[rs, C_DK:C_DK + DIFF_W][:, grp]
            for c in range(2):
                own = low_half if c == 0 else jnp.logical_not(low_half)
                inst = FOX_HEADS + 2 * hd + c
                ka_ref[0, inst, rs, :] = jnp.where(own, k, ek).astype(BF16)
                qat_ref[0, inst, s] = jnp.where(own, q, eq).T.astype(BF16)
        for g in range(FOX_W // LANE):
            vt_ref[0, s, g * LANE:(g + 1) * LANE, :] = (
                proj[rs, C_FV + g * LANE:C_FV + (g + 1) * LANE].T.astype(BF16))
        for g in range(DIFF_W // LANE):
            vt_ref[0, s, FOX_W + g * LANE:FOX_W + (g + 1) * LANE, :] = (
                proj[rs, C_DV + g * LANE:C_DV + (g + 1) * LANE].T.astype(BF16))
    carry_ref[...] = jnp.broadcast_to(carry, carry_ref.shape)

    u_ref[halo:halo + tm, :] = pu
    acc = pu
    sums = {}
    for k in range(1, halo):
        acc = acc + u_ref[halo - k:halo - k + tm, :]
        if k + 1 in POOL_WINDOWS:
            sums[k + 1] = acc
    prow = i * tm + lax.broadcasted_iota(jnp.int32, (tm, POOL_W), 0)
    plane = lax.broadcasted_iota(jnp.int32, (tm, POOL_W), 1)
    pos1 = prow - (FRONT - 1)
    gd = POOL_W // len(POOL_WINDOWS)
    mean = None
    for gi, wdw in reversed(list(enumerate(POOL_WINDOWS))):
        cand = sums[wdw] / jnp.clip(pos1, 1, wdw).astype(F32)
        mean = cand if mean is None else jnp.where(plane < (gi + 1) * gd, cand, mean)
    pooled = mean - pu
    y = jnp.dot(pooled.astype(BF16), wp_ref[...], preferred_element_type=F32)
    cp_ref[0] = (y * ps_ref[...] * jax.nn.silu(pz)).astype(BF16)
    u_ref[0:halo, :] = pu[tm - halo:tm, :]


def _proj_call(h, g, w_all, layer, bfp, wp, ps, tm):
    B, Lq, D = h.shape
    nt = Lq // tm
    cpt = tm // LANE
    row_spec = lambda width: pl.BlockSpec((1, tm, width), lambda b, i: (b, i, 0))
    full = lambda a: pl.BlockSpec(a.shape, lambda b, i: (0,) * a.ndim)
    return pl.pallas_call(
        _proj_kernel,
        grid=(B, nt),
        in_specs=[row_spec(D), full(g),
                  pl.BlockSpec((1, D, W_COLS), lambda b, i: (layer, 0, 0)),
                  full(bfp), full(wp), full(ps)],
        out_specs=[pl.BlockSpec((1, N_INST, tm, LANE), lambda b, i: (b, 0, i, 0)),
                   pl.BlockSpec((1, N_INST, cpt, LANE, LANE), lambda b, i: (b, 0, i, 0, 0)),
                   pl.BlockSpec((1, cpt, ATT_W, LANE), lambda b, i: (b, i, 0, 0)),
                   row_spec(ATT_W), row_spec(POOL_W)],
        out_shape=[jax.ShapeDtypeStruct((B, N_INST, Lq, LANE), BF16),
                   jax.ShapeDtypeStruct((B, N_INST, Lq // LANE, LANE, LANE), BF16),
                   jax.ShapeDtypeStruct((B, Lq // LANE, ATT_W, LANE), BF16),
                   jax.ShapeDtypeStruct((B, Lq, ATT_W), BF16),
                   jax.ShapeDtypeStruct((B, Lq, POOL_W), BF16)],
        scratch_shapes=[pltpu.VMEM((8, LANE), F32),
                        pltpu.VMEM((POOL_WINDOWS[-1] + tm, POOL_W), F32)],
        compiler_params=pltpu.CompilerParams(
            dimension_semantics=("arbitrary", "arbitrary"), vmem_limit_bytes=VMEM_LIMIT),
        name="proj",
    )(h, g, w_all, bfp, wp, ps)


def _attn_kernel(*refs, diff, lambda_init):
    if diff:
        ka_ref, qat_ref, vt_ref, lq1_ref, lk1_ref, lq2_ref, lk2_ref, sg_ref, ot_ref, dm_ref = refs
    else:
        ka_ref, qat_ref, vt_ref, ot_ref, dm_ref = refs
    lq = ka_ref.shape[2]
    n_sblocks = (lq - LANE) // (QSUB * TQ)
    pid = pl.program_id(1)
    dv = LANE if diff else HEAD
    kc = TQ // LANE

    kk = lax.broadcasted_iota(jnp.int32, (TQ, TQ), 0)
    qq = lax.broadcasted_iota(jnp.int32, (TQ, TQ), 1)
    if diff:
        for hh in range(NI // 2):
            slope = jnp.where(pid == 0, ALIBI_SLOPES[hh], ALIBI_SLOPES[NI // 2 + hh]).astype(F32)
            fix = jnp.where(kk > qq, (2.0 * LOG2E * slope) * (qq - kk).astype(F32), 0.0)
            dm_ref[hh] = jnp.where(kk // CHUNK > qq // CHUNK, MASK, fix)
        lam = (jnp.exp(jnp.sum(lq1_ref[...] * lk1_ref[...], axis=-1, keepdims=True))
               - jnp.exp(jnp.sum(lq2_ref[...] * lk2_ref[...], axis=-1, keepdims=True))
               + lambda_init)
    else:
        dm_ref[0] = jnp.where(kk > qq, MASK, 0.0)

    def vrows(inst):
        return slice(dv * (inst // 2), dv * (inst // 2 + 1)) if diff else slice(dv * inst, dv * (inst + 1))

    def dm(inst, rows=slice(None), cols=slice(None)):
        return dm_ref[inst // 2 if diff else 0, rows, cols]

    def scores(inst, qat, key_start, n_key_chunks):
        ka = ka_ref[0, inst, pl.ds(key_start, n_key_chunks * LANE), :]
        return jnp.dot(ka, qat, preferred_element_type=F32)

    def partial(inst, s, first_chunk):
        m = jnp.max(s, axis=0, keepdims=True)
        p = jnp.exp2(s - m)
        l = jnp.sum(p, axis=0, keepdims=True)
        pb = p.astype(BF16)
        pv = None
        for cch in range(s.shape[0] // LANE):
            vt = vt_ref[0, first_chunk + cch, vrows(inst), :]
            d = jnp.dot(vt, pb[cch * LANE:(cch + 1) * LANE], preferred_element_type=F32)
            pv = d if pv is None else pv + d
        return m, l, pv

    def merge(state, parts):
        m, l, acc = state
        m_new = m
        for mp, _, _ in parts:
            m_new = jnp.maximum(m_new, mp)
        w = jnp.exp2(m - m_new)
        l = w * l
        acc = w * acc
        for mp, lp, pvp in parts:
            wp = jnp.exp2(mp - m_new)
            l = l + wp * lp
            acc = acc + wp * pvp
        return m_new, l, acc

    def init_state(tq):
        return (jnp.full((1, tq), M_INIT, F32), jnp.zeros((1, tq), F32), jnp.zeros((dv, tq), F32))

    def finish(states, first_chunk, n_q_chunks):
        if diff:
            heads = []
            for hh in range(NI // 2):
                (_, l0, a0), (_, l1, a1) = states[2 * hh], states[2 * hh + 1]
                o = a0 / l0 - lam * (a1 / l1)
                o = o * lax.rsqrt(jnp.mean(o * o, axis=0, keepdims=True) + SUBLN_EPS) * sg_ref[...]
                heads.append(o * (1.0 - lambda_init))
        else:
            heads = [acc / l for (_, l, acc) in states]
        o = jnp.concatenate(heads, axis=0)
        for cch in range(n_q_chunks):
            ot_ref[0, first_chunk + cch] = o[:, cch * LANE:(cch + 1) * LANE]

    lead = []
    for inst in range(NI):
        s = scores(inst, qat_ref[0, inst, 0], 0, 1) + dm(inst, slice(0, LANE), slice(0, LANE))
        lead.append(merge(init_state(LANE), [partial(inst, s, 0)]))
    finish(lead, 0, 1)

    pairs = [(inst, qs) for inst in range(NI) for qs in range(QSUB)]

    def superblock(i, _):
        qats = {}
        for inst, qs in pairs:
            c0 = 1 + kc * (QSUB * i + qs)
            qats[inst, qs] = jnp.concatenate([qat_ref[0, inst, c0 + c] for c in range(kc)], axis=1)

        def kvstep(j, carry):
            k_start = pl.multiple_of(LANE + j * (QSUB * TQ), LANE)
            out = []
            for n, (inst, qs) in enumerate(pairs):
                s = scores(inst, qats[inst, qs], k_start, QSUB * kc)
                parts = [partial(inst, s[u * TQ:(u + 1) * TQ], 1 + kc * (QSUB * j + u)) for u in range(QSUB)]
                out.extend(merge(carry[3 * n:3 * n + 3], parts))
            return tuple(out)

        carry0 = tuple(x for _ in pairs for x in init_state(TQ))
        carry = lax.fori_loop(0, i, kvstep, carry0)

        d_start = pl.multiple_of(LANE + i * (QSUB * TQ), LANE)
        for qs in range(QSUB):
            states = []
            for inst in range(NI):
                n = pairs.index((inst, qs))
                parts = [partial(inst, scores(inst, qats[inst, qs], 0, 1), 0)]
                s = scores(inst, qats[inst, qs], d_start, (qs + 1) * kc)
                for u in range(qs + 1):
                    su = s[u * TQ:(u + 1) * TQ]
                    if u == qs:
                        su = su + dm(inst)
                    parts.append(partial(inst, su, 1 + kc * (QSUB * i + u)))
                states.append(merge(carry[3 * n:3 * n + 3], parts))
            finish(states, 1 + kc * (QSUB * i + qs), kc)
        return 0

    lax.fori_loop(0, n_sblocks, superblock, 0)


def _attn_call(ka, qat, vt, extra, *, diff, lambda_init):
    B, _, Lq, _ = ka.shape
    nc = Lq // LANE
    n_groups = 2 * DIFF_HEADS // NI if diff else FOX_HEADS // NI
    first = FOX_HEADS // NI if diff else 0
    in_specs = [pl.BlockSpec((1, NI, Lq, LANE), lambda b, g: (b, first + g, 0, 0)),
                pl.BlockSpec((1, NI, nc, LANE, LANE), lambda b, g: (b, first + g, 0, 0, 0)),
                pl.BlockSpec((1, nc, GROUP_W, LANE), lambda b, g: (b, 0, first + g, 0))]
    args = [ka, qat, vt]
    for a in extra:
        in_specs.append(pl.BlockSpec(a.shape, lambda b, g, nd=a.ndim: (0,) * nd))
        args.append(a)
    return pl.pallas_call(
        functools.partial(_attn_kernel, diff=diff, lambda_init=lambda_init),
        grid=(B, n_groups),
        in_specs=in_specs,
        out_specs=pl.BlockSpec((1, nc, GROUP_W, LANE), lambda b, g: (b, 0, g, 0)),
        out_shape=jax.ShapeDtypeStruct((B, nc, n_groups * GROUP_W, LANE), F32),
        scratch_shapes=[pltpu.VMEM((NI // 2 if diff else 1, TQ, TQ), F32)],
        compiler_params=pltpu.CompilerParams(
            dimension_semantics=("arbitrary", "arbitrary"), vmem_limit_bytes=VMEM_LIMIT),
        name="diff_attn" if diff else "fox_attn",
    )(*args)


def _out_kernel(h_ref, oa_ref, ob_ref, z_ref, c_ref, w_ref, *rest, final):
    if final:
        g_ref, o_ref = rest
    else:
        (o_ref,) = rest
    tm = h_ref.shape[1]
    rows = []
    for s in range(tm // LANE):
        pieces = [oa_ref[0, s, g * LANE:(g + 1) * LANE, :].T for g in range(FOX_W // LANE)]
        pieces += [ob_ref[0, s, g * LANE:(g + 1) * LANE, :].T for g in range(DIFF_W // LANE)]
        att = jnp.concatenate(pieces, axis=1)
        gate = jax.nn.silu(z_ref[0, s * LANE:(s + 1) * LANE, :].astype(F32))
        rows.append((att * gate).astype(BF16))
    mix = rows[0] if len(rows) == 1 else jnp.concatenate(rows, axis=0)
    hn = (h_ref[0]
          + jnp.dot(mix, w_ref[0:ATT_W, :], preferred_element_type=F32)
          + jnp.dot(c_ref[0], w_ref[ATT_W:, :], preferred_element_type=F32))
    if final:
        hn = hn * lax.rsqrt(jnp.mean(hn * hn, axis=-1, keepdims=True) + NORM_EPS) * g_ref[...]
    o_ref[0] = hn


def _out_call(h, oa, ob, z, c, w, tm):
    B, Lq, D = h.shape
    cpt = tm // LANE
    row_spec = lambda width: pl.BlockSpec((1, tm, width), lambda bb, i: (bb, i, 0))
    return pl.pallas_call(
        functools.partial(_out_kernel, final=False),
        grid=(B, Lq // tm),
        in_specs=[row_spec(D),
                  pl.BlockSpec((1, cpt, FOX_W, LANE), lambda bb, i: (bb, i, 0, 0)),
                  pl.BlockSpec((1, cpt, DIFF_W, LANE), lambda bb, i: (bb, i, 0, 0)),
                  row_spec(ATT_W), row_spec(POOL_W),
                  pl.BlockSpec(w.shape, lambda bb, i: (0, 0))],
        out_specs=row_spec(D),
        out_shape=jax.ShapeDtypeStruct((B, Lq, D), F32),
        compiler_params=pltpu.CompilerParams(
            dimension_semantics=("arbitrary", "arbitrary"), vmem_limit_bytes=VMEM_LIMIT),
        name="out_proj",
    )(h, oa, ob, z, c, w)


def _final_call(h, oa, ob, z, c, w, g):
    B, Lq, D = h.shape
    S = Lq - LANE
    row_spec = lambda width: pl.BlockSpec((1, LANE, width), lambda bb, i: (bb, i + 1, 0))
    return pl.pallas_call(
        functools.partial(_out_kernel, final=True),
        grid=(B, S // LANE),
        in_specs=[row_spec(D),
                  pl.BlockSpec((1, 1, FOX_W, LANE), lambda bb, i: (bb, i + 1, 0, 0)),
                  pl.BlockSpec((1, 1, DIFF_W, LANE), lambda bb, i: (bb, i + 1, 0, 0)),
                  row_spec(ATT_W), row_spec(POOL_W),
                  pl.BlockSpec(w.shape, lambda bb, i: (0, 0)),
                  pl.BlockSpec(g.shape, lambda bb, i: (0, 0))],
        out_specs=pl.BlockSpec((1, LANE, D), lambda bb, i: (bb, i, 0)),
        out_shape=jax.ShapeDtypeStruct((B, S, D), F32),
        compiler_params=pltpu.CompilerParams(
            dimension_semantics=("arbitrary", "arbitrary"), vmem_limit_bytes=VMEM_LIMIT),
        name="out_proj_final",
    )(h, oa, ob, z, c, w, g)


def _row_tile(lq):
    for tm in (384, 256, 128):
        if lq % tm == 0:
            return tm
    raise ValueError(f"sequence of {lq} rows is not a multiple of {LANE}")


def kernel(x, meta_tokens, norm_g, w_in, b_f, lam_q1, lam_k1, lam_q2, lam_k2, subln_g, w_pool,
           pool_scale, w_out, final_g):
    B, S, D = x.shape
    depth = w_in.shape[0]
    assert S % (QSUB * TQ) == 0 and D == HEAD_W and w_in.shape[2] == HEAD_W + FOX_HEADS + TAIL_W
    Lq = LANE + S
    tm = _row_tile(Lq)
    meta = jnp.broadcast_to(meta_tokens.astype(x.dtype)[None], (B, N_META, D))
    h = jnp.concatenate([jnp.zeros((B, FRONT, D), x.dtype), meta, x], axis=1)
    w_all = _arrange_call(w_in)

    y = None
    for l in range(depth):
        lambda_init = 0.8 - 0.6 * math.exp(-0.3 * l)
        bfp = jnp.pad(b_f[l].astype(F32), (FORGET_LANE, 0))[None]
        wp = jax.scipy.linalg.block_diag(*[w_pool[l, g] for g in range(len(POOL_WINDOWS))]).astype(BF16)
        ka, qat, vt, z, cp = _proj_call(h, norm_g[l][None].astype(F32), w_all, l, bfp, wp,
                                        pool_scale[l][None].astype(F32), tm)
        oa = _attn_call(ka, qat, vt, (), diff=False, lambda_init=lambda_init)
        extra = (lam_q1[l][None].astype(F32), lam_k1[l][None].astype(F32),
                 lam_q2[l][None].astype(F32), lam_k2[l][None].astype(F32),
                 subln_g[l][:, None].astype(F32))
        ob = _attn_call(ka, qat, vt, extra, diff=True, lambda_init=lambda_init)
        wo = w_out[l].astype(BF16)
        if l + 1 < depth:
            h = _out_call(h, oa, ob, z, cp, wo, tm)
        else:
            y = _final_call(h, oa, ob, z, cp, wo, final_g[None].astype(F32))
    return y
```

```python
import functools
import math

import jax
import jax.numpy as jnp
from jax import lax
from jax.experimental import pallas as pl
from jax.experimental.pallas import tpu as pltpu

F32 = jnp.float32
BF16 = jnp.bfloat16

LANE = 128
N_META = 16
CHUNK = 64
FRONT = LANE - N_META
HEAD = 64
FOX_HEADS = 4
DIFF_HEADS = 4
N_INST = FOX_HEADS + 2 * DIFF_HEADS
FOX_W = 256
DIFF_W = 512
ATT_W = FOX_W + DIFF_W
POOL_W = 256
POOL_WINDOWS = (2, 4, 8, 16)
ALIBI_SLOPES = tuple(2.0 ** (-8.0 * (h + 1) / DIFF_HEADS) for h in range(DIFF_HEADS))
NORM_EPS = 1e-6
SUBLN_EPS = 1e-5
MASK = -1e30
M_INIT = -1e38
TQ = 256
QSUB = 4
CHAIN_KEYS = 128
NI = 4
GROUP_W = NI * HEAD
LOG2E = 1.4426950408889634
VMEM_LIMIT = 56 * 1024 * 1024

HEAD_W = 4 * FOX_W
TAIL_W = 4 * DIFF_W + 2 * POOL_W
TAIL_PAD = TAIL_W + LANE
W_COLS = HEAD_W + TAIL_PAD
FORGET_LANE = LANE - FOX_HEADS
C_FQ, C_FK, C_FV, C_FZ = 0, FOX_W, 2 * FOX_W, 3 * FOX_W
C_DQ, C_DK, C_DV, C_DZ = HEAD_W, HEAD_W + DIFF_W, HEAD_W + 2 * DIFF_W, HEAD_W + 3 * DIFF_W
C_PU, C_PZ, C_FG = HEAD_W + 4 * DIFF_W, HEAD_W + 4 * DIFF_W + POOL_W, HEAD_W + TAIL_W


def _split3(x):
    hi = x.astype(BF16)
    r1 = x - hi.astype(F32)
    mid = r1.astype(BF16)
    lo = (r1 - mid.astype(F32)).astype(BF16)
    return hi, mid, lo


def _arrange_kernel(head_ref, tail_ref, o_ref):
    o_ref[0, :, 0:HEAD_W] = head_ref[0].astype(BF16)
    o_ref[0, :, HEAD_W:] = pltpu.roll(tail_ref[0], TAIL_PAD - FOX_HEADS, axis=1).astype(BF16)


def _arrange_call(w_in):
    depth, d, n = w_in.shape
    head = w_in[:, :, 0:HEAD_W]
    tail = jnp.pad(w_in[:, :, HEAD_W:], ((0, 0), (0, 0), (0, TAIL_PAD - (n - HEAD_W))))
    rows = 256
    return pl.pallas_call(
        _arrange_kernel,
        grid=(depth, d // rows),
        in_specs=[pl.BlockSpec((1, rows, HEAD_W), lambda l, i: (l, i, 0)),
                  pl.BlockSpec((1, rows, TAIL_PAD), lambda l, i: (l, i, 0))],
        out_specs=pl.BlockSpec((1, rows, W_COLS), lambda l, i: (l, i, 0)),
        out_shape=jax.ShapeDtypeStruct((depth, d, W_COLS), BF16),
        compiler_params=pltpu.CompilerParams(
            dimension_semantics=("arbitrary", "arbitrary"), vmem_limit_bytes=VMEM_LIMIT),
        name="arrange_w",
    )(head, tail)


def _proj_kernel(h_ref, g_ref, w_ref, bf_ref, wp_ref, ps_ref,
                 ka_ref, qat_ref, vt_ref, z_ref, cp_ref, carry_ref, u_ref):
    i = pl.program_id(1)
    tm = h_ref.shape[1]
    halo = POOL_WINDOWS[-1]

    @pl.when(i == 0)
    def _():
        carry_ref[...] = jnp.zeros_like(carry_ref)
        u_ref[0:halo, :] = jnp.zeros((halo, POOL_W), F32)

    h = h_ref[0]
    hn = h * lax.rsqrt(jnp.mean(h * h, axis=-1, keepdims=True) + NORM_EPS) * g_ref[...]
    proj = jnp.dot(hn.astype(BF16), w_ref[0], preferred_element_type=F32)
    z_ref[0, :, 0:FOX_W] = proj[:, C_FZ:C_FZ + FOX_W].astype(BF16)
    z_ref[0, :, FOX_W:] = proj[:, C_DZ:C_DZ + DIFF_W].astype(BF16)
    pu = proj[:, C_PU:C_PU + POOL_W]
    pz = proj[:, C_PZ:C_PZ + POOL_W]
    fg = proj[:, C_FG:]

    row = i * tm + lax.broadcasted_iota(jnp.int32, (tm, LANE), 0)
    logf = jnp.where(row >= FRONT, jax.nn.log_sigmoid(fg + bf_ref[...]), 0.0)
    tri = (lax.broadcasted_iota(jnp.int32, (LANE, LANE), 0)
           >= lax.broadcasted_iota(jnp.int32, (LANE, LANE), 1)).astype(BF16)
    lane = lax.broadcasted_iota(jnp.int32, (LANE, LANE), 1)
    lm = lane & (HEAD - 1)
    low_half = lane < HEAD
    rloc = lax.broadcasted_iota(jnp.int32, (LANE, 1), 0)

    def extras(q_bias, dead):
        qh, qm, ql = (p.astype(F32) for p in _split3(q_bias))
        one_q = jnp.where((lm >= 3) & (lm < 7), 1.0, 0.0)
        eq = jnp.where(lm == 0, qh, jnp.where(lm == 1, qm, jnp.where(lm == 2, ql, one_q)))
        one_k = jnp.where(lm < 3, 1.0, 0.0)
        ek = jnp.where(lm == 3, -qh, jnp.where(lm == 4, -qm, jnp.where(lm == 5, -ql,
                                                                       jnp.where(lm == 6, dead, one_k))))
        return eq, ek

    carry = carry_ref[0:1, :]
    for s in range(tm // LANE):
        rs = slice(s * LANE, (s + 1) * LANE)
        parts = _split3(logf[rs])
        within = sum(jnp.dot(tri, p, preferred_element_type=F32) for p in parts)
        cblk = within + carry
        carry = cblk[LANE - 1:LANE, :]

        grow = i * tm + s * LANE + rloc
        dead = jnp.where(grow < FRONT, MASK, 0.0)
        for hd in range(FOX_HEADS):
            cb = LOG2E * cblk[:, FORGET_LANE + hd:FORGET_LANE + hd + 1]
            eq, ek = extras(cb, dead)
            grp = slice((hd // 2) * LANE, (hd // 2 + 1) * LANE)
            own = low_half if hd % 2 == 0 else jnp.logical_not(low_half)
            q = proj[rs, C_FQ:C_FQ + FOX_W][:, grp] * (HEAD ** -0.5 * LOG2E)
            k = proj[rs, C_FK:C_FK + FOX_W][:, grp]
            ka_ref[0, hd, rs, :] = jnp.where(own, k, ek).astype(BF16)
            qat_ref[0, hd, s] = jnp.where(own, q, eq).T.astype(BF16)
        for hd in range(DIFF_HEADS):
            tb = (-LOG2E * ALIBI_SLOPES[hd]) * grow.astype(F32)
            eq, ek = extras(tb, dead)
            grp = slice(hd * LANE, (hd + 1) * LANE)
            q = proj[rs, C_DQ:C_DQ + DIFF_W][:, grp] * (HEAD ** -0.5 * LOG2E)
            k = proj[rs, C_DK:C_DK + DIFF_W][:, grp]
            for c in range(2):
                own = low_half if c == 0 else jnp.logical_not(low_half)
                inst = FOX_HEADS + 2 * hd + c
                ka_ref[0, inst, rs, :] = jnp.where(own, k, ek).astype(BF16)
                qat_ref[0, inst, s] = jnp.where(own, q, eq).T.astype(BF16)
        for g in range(FOX_W // LANE):
            vt_ref[0, s, g * LANE:(g + 1) * LANE, :] = (
                proj[rs, C_FV + g * LANE:C_FV + (g + 1) * LANE].T.astype(BF16))
        for g in range(DIFF_W // LANE):
            vt_ref[0, s, FOX_W + g * LANE:FOX_W + (g + 1) * LANE, :] = (
                proj[rs, C_DV + g * LANE:C_DV + (g + 1) * LANE].T.astype(BF16))
    carry_ref[...] = jnp.broadcast_to(carry, carry_ref.shape)

    u_ref[halo:halo + tm, :] = pu
    acc = pu
    sums = {}
    for k in range(1, halo):
        acc = acc + u_ref[halo - k:halo - k + tm, :]
        if k + 1 in POOL_WINDOWS:
            sums[k + 1] = acc
    prow = i * tm + lax.broadcasted_iota(jnp.int32, (tm, POOL_W), 0)
    plane = lax.broadcasted_iota(jnp.int32, (tm, POOL_W), 1)
    pos1 = prow - (FRONT - 1)
    gd = POOL_W // len(POOL_WINDOWS)
    mean = None
    for gi, wdw in reversed(list(enumerate(POOL_WINDOWS))):
        cand = sums[wdw] / jnp.clip(pos1, 1, wdw).astype(F32)
        mean = cand if mean is None else jnp.where(plane < (gi + 1) * gd, cand, mean)
    pooled = mean - pu
    y = jnp.dot(pooled.astype(BF16), wp_ref[...], preferred_element_type=F32)
    cp_ref[0] = (y * ps_ref[...] * jax.nn.silu(pz)).astype(BF16)
    u_ref[0:halo, :] = pu[tm - halo:tm, :]


def _proj_call(h, g, w_all, layer, bfp, wp, ps, tm):
    B, Lq, D = h.shape
    nt = Lq // tm
    cpt = tm // LANE
    row_spec = lambda width: pl.BlockSpec((1, tm, width), lambda b, i: (b, i, 0))
    full = lambda a: pl.BlockSpec(a.shape, lambda b, i: (0,) * a.ndim)
    return pl.pallas_call(
        _proj_kernel,
        grid=(B, nt),
        in_specs=[row_spec(D), full(g),
                  pl.BlockSpec((1, D, W_COLS), lambda b, i: (layer, 0, 0)),
                  full(bfp), full(wp), full(ps)],
        out_specs=[pl.BlockSpec((1, N_INST, tm, LANE), lambda b, i: (b, 0, i, 0)),
                   pl.BlockSpec((1, N_INST, cpt, LANE, LANE), lambda b, i: (b, 0, i, 0, 0)),
                   pl.BlockSpec((1, cpt, ATT_W, LANE), lambda b, i: (b, i, 0, 0)),
                   row_spec(ATT_W), row_spec(POOL_W)],
        out_shape=[jax.ShapeDtypeStruct((B, N_INST, Lq, LANE), BF16),
                   jax.ShapeDtypeStruct((B, N_INST, Lq // LANE, LANE, LANE), BF16),
                   jax.ShapeDtypeStruct((B, Lq // LANE, ATT_W, LANE), BF16),
                   jax.ShapeDtypeStruct((B, Lq, ATT_W), BF16),
                   jax.ShapeDtypeStruct((B, Lq, POOL_W), BF16)],
        scratch_shapes=[pltpu.VMEM((8, LANE), F32),
                        pltpu.VMEM((POOL_WINDOWS[-1] + tm, POOL_W), F32)],
        compiler_params=pltpu.CompilerParams(
            dimension_semantics=("arbitrary", "arbitrary"), vmem_limit_bytes=VMEM_LIMIT),
        name="proj",
    )(h, g, w_all, bfp, wp, ps)


def _attn_kernel(*refs, diff, lambda_init):
    if diff:
        ka_ref, qat_ref, vt_ref, lq1_ref, lk1_ref, lq2_ref, lk2_ref, sg_ref, ot_ref, dm_ref = refs
    else:
        ka_ref, qat_ref, vt_ref, ot_ref, dm_ref = refs
    lq = ka_ref.shape[2]
    n_sblocks = (lq - LANE) // (QSUB * TQ)
    pid = pl.program_id(1)
    dv = LANE if diff else HEAD
    kc = TQ // LANE

    kk = lax.broadcasted_iota(jnp.int32, (TQ, TQ), 0)
    qq = lax.broadcasted_iota(jnp.int32, (TQ, TQ), 1)
    if diff:
        for hh in range(NI // 2):
            slope = jnp.where(pid == 0, ALIBI_SLOPES[hh], ALIBI_SLOPES[NI // 2 + hh]).astype(F32)
            fix = jnp.where(kk > qq, (2.0 * LOG2E * slope) * (qq - kk).astype(F32), 0.0)
            dm_ref[hh] = jnp.where(kk // CHUNK > qq // CHUNK, MASK, fix)
        lam = (jnp.exp(jnp.sum(lq1_ref[...] * lk1_ref[...], axis=-1, keepdims=True))
               - jnp.exp(jnp.sum(lq2_ref[...] * lk2_ref[...], axis=-1, keepdims=True))
               + lambda_init)
    else:
        dm_ref[0] = jnp.where(kk > qq, MASK, 0.0)

    def vrows(inst):
        return slice(LANE * (inst // 2), LANE * (inst // 2 + 1))

    def dm(inst, rows=slice(None), cols=slice(None)):
        return dm_ref[inst // 2 if diff else 0, rows, cols]

    def scores(inst, qat, key_start, n_key_chunks):
        ka = ka_ref[0, inst, pl.ds(key_start, n_key_chunks * LANE), :]
        return jnp.dot(ka, qat, preferred_element_type=F32)

    def partial(inst, s, first_chunk):
        m = jnp.max(s, axis=0, keepdims=True)
        p = jnp.exp2(s - m)
        l = jnp.sum(p, axis=0, keepdims=True)
        pb = p.astype(BF16)
        pv = None
        for cch in range(s.shape[0] // LANE):
            vt = vt_ref[0, first_chunk + cch, vrows(inst), :]
            d = jnp.dot(vt, pb[cch * LANE:(cch + 1) * LANE], preferred_element_type=F32)
            pv = d if pv is None else pv + d
        if not diff:
            pv = pv[HEAD * (inst % 2):HEAD * (inst % 2 + 1)]
        return m, l, pv

    def merge(state, parts):
        m, l, acc = state
        m_new = m
        for mp, _, _ in parts:
            m_new = jnp.maximum(m_new, mp)
        w = jnp.exp2(m - m_new)
        l = w * l
        acc = w * acc
        for mp, lp, pvp in parts:
            wp = jnp.exp2(mp - m_new)
            l = l + wp * lp
            acc = acc + wp * pvp
        return m_new, l, acc

    def init_state(tq):
        return (jnp.full((1, tq), M_INIT, F32), jnp.zeros((1, tq), F32), jnp.zeros((dv, tq), F32))

    def finish(states, first_chunk, n_q_chunks):
        if diff:
            heads = []
            for hh in range(NI // 2):
                (_, l0, a0), (_, l1, a1) = states[2 * hh], states[2 * hh + 1]
                o = a0 / l0 - lam * (a1 / l1)
                o = o * lax.rsqrt(jnp.mean(o * o, axis=0, keepdims=True) + SUBLN_EPS) * sg_ref[...]
                heads.append(o * (1.0 - lambda_init))
        else:
            heads = [acc / l for (_, l, acc) in states]
        o = jnp.concatenate(heads, axis=0)
        for cch in range(n_q_chunks):
            ot_ref[0, first_chunk + cch] = o[:, cch * LANE:(cch + 1) * LANE]

    lead = []
    for inst in range(NI):
        s = scores(inst, qat_ref[0, inst, 0], 0, 1) + dm(inst, slice(0, LANE), slice(0, LANE))
        lead.append(merge(init_state(LANE), [partial(inst, s, 0)]))
    finish(lead, 0, 1)

    pairs = [(inst, qs) for inst in range(NI) for qs in range(QSUB)]

    def superblock(i, _):
        qats = {}
        for inst, qs in pairs:
            c0 = 1 + kc * (QSUB * i + qs)
            qats[inst, qs] = jnp.concatenate([qat_ref[0, inst, c0 + c] for c in range(kc)], axis=1)

        def kvstep(j, carry):
            k_start = pl.multiple_of(LANE + j * (QSUB * TQ), LANE)
            out = []
            for n, (inst, qs) in enumerate(pairs):
                s = scores(inst, qats[inst, qs], k_start, QSUB * kc)
                parts = [partial(inst, s[u * CHAIN_KEYS:(u + 1) * CHAIN_KEYS],
                                 1 + kc * QSUB * j + u * (CHAIN_KEYS // LANE))
                         for u in range(QSUB * TQ // CHAIN_KEYS)]
                out.extend(merge(carry[3 * n:3 * n + 3], parts))
            return tuple(out)

        carry0 = tuple(x for _ in pairs for x in init_state(TQ))
        carry = lax.fori_loop(0, i, kvstep, carry0)

        d_start = pl.multiple_of(LANE + i * (QSUB * TQ), LANE)
        for qs in range(QSUB):
            states = []
            for inst in range(NI):
                n = pairs.index((inst, qs))
                parts = [partial(inst, scores(inst, qats[inst, qs], 0, 1), 0)]
                s = scores(inst, qats[inst, qs], d_start, (qs + 1) * kc)
                for u in range(qs + 1):
                    su = s[u * TQ:(u + 1) * TQ]
                    if u == qs:
                        su = su + dm(inst)
                    parts.append(partial(inst, su, 1 + kc * (QSUB * i + u)))
                states.append(merge(carry[3 * n:3 * n + 3], parts))
            finish(states, 1 + kc * (QSUB * i + qs), kc)
        return 0

    lax.fori_loop(0, n_sblocks, superblock, 0)


def _attn_call(ka, qat, vt, extra, *, diff, lambda_init):
    B, _, Lq, _ = ka.shape
    nc = Lq // LANE
    n_groups = 2 * DIFF_HEADS // NI if diff else FOX_HEADS // NI
    first = FOX_HEADS // NI if diff else 0
    in_specs = [pl.BlockSpec((1, NI, Lq, LANE), lambda b, g: (b, first + g, 0, 0)),
                pl.BlockSpec((1, NI, nc, LANE, LANE), lambda b, g: (b, first + g, 0, 0, 0)),
                pl.BlockSpec((1, nc, GROUP_W, LANE), lambda b, g: (b, 0, first + g, 0))]
    args = [ka, qat, vt]
    for a in extra:
        in_specs.append(pl.BlockSpec(a.shape, lambda b, g, nd=a.ndim: (0,) * nd))
        args.append(a)
    return pl.pallas_call(
        functools.partial(_attn_kernel, diff=diff, lambda_init=lambda_init),
        grid=(B, n_groups),
        in_specs=in_specs,
        out_specs=pl.BlockSpec((1, nc, GROUP_W, LANE), lambda b, g: (b, 0, g, 0)),
        out_shape=jax.ShapeDtypeStruct((B, nc, n_groups * GROUP_W, LANE), F32),
        scratch_shapes=[pltpu.VMEM((NI // 2 if diff else 1, TQ, TQ), F32)],
        compiler_params=pltpu.CompilerParams(
            dimension_semantics=("arbitrary", "arbitrary"), vmem_limit_bytes=VMEM_LIMIT),
        name="diff_attn" if diff else "fox_attn",
    )(*args)


def _out_kernel(h_ref, oa_ref, ob_ref, z_ref, c_ref, w_ref, *rest, final):
    if final:
        g_ref, o_ref = rest
    else:
        (o_ref,) = rest
    tm = h_ref.shape[1]
    rows = []
    for s in range(tm // LANE):
        pieces = [oa_ref[0, s, g * LANE:(g + 1) * LANE, :].T for g in range(FOX_W // LANE)]
        pieces += [ob_ref[0, s, g * LANE:(g + 1) * LANE, :].T for g in range(DIFF_W // LANE)]
        att = jnp.concatenate(pieces, axis=1)
        gate = jax.nn.silu(z_ref[0, s * LANE:(s + 1) * LANE, :].astype(F32))
        rows.append((att * gate).astype(BF16))
    mix = rows[0] if len(rows) == 1 else jnp.concatenate(rows, axis=0)
    hn = (h_ref[0]
          + jnp.dot(mix, w_ref[0:ATT_W, :], preferred_element_type=F32)
          + jnp.dot(c_ref[0], w_ref[ATT_W:, :], preferred_element_type=F32))
    if final:
        hn = hn * lax.rsqrt(jnp.mean(hn * hn, axis=-1, keepdims=True) + NORM_EPS) * g_ref[...]
    o_ref[0] = hn


def _out_call(h, oa, ob, z, c, w, tm):
    B, Lq, D = h.shape
    cpt = tm // LANE
    row_spec = lambda width: pl.BlockSpec((1, tm, width), lambda bb, i: (bb, i, 0))
    return pl.pallas_call(
        functools.partial(_out_kernel, final=False),
        grid=(B, Lq // tm),
        in_specs=[row_spec(D),
                  pl.BlockSpec((1, cpt, FOX_W, LANE), lambda bb, i: (bb, i, 0, 0)),
                  pl.BlockSpec((1, cpt, DIFF_W, LANE), lambda bb, i: (bb, i, 0, 0)),
                  row_spec(ATT_W), row_spec(POOL_W),
                  pl.BlockSpec(w.shape, lambda bb, i: (0, 0))],
        out_specs=row_spec(D),
        out_shape=jax.ShapeDtypeStruct((B, Lq, D), F32),
        compiler_params=pltpu.CompilerParams(
            dimension_semantics=("arbitrary", "arbitrary"), vmem_limit_bytes=VMEM_LIMIT),
        name="out_proj",
    )(h, oa, ob, z, c, w)


def _final_call(h, oa, ob, z, c, w, g):
    B, Lq, D = h.shape
    S = Lq - LANE
    row_spec = lambda width: pl.BlockSpec((1, LANE, width), lambda bb, i: (bb, i + 1, 0))
    return pl.pallas_call(
        functools.partial(_out_kernel, final=True),
        grid=(B, S // LANE),
        in_specs=[row_spec(D),
                  pl.BlockSpec((1, 1, FOX_W, LANE), lambda bb, i: (bb, i + 1, 0, 0)),
                  pl.BlockSpec((1, 1, DIFF_W, LANE), lambda bb, i: (bb, i + 1, 0, 0)),
                  row_spec(ATT_W), row_spec(POOL_W),
                  pl.BlockSpec(w.shape, lambda bb, i: (0, 0)),
                  pl.BlockSpec(g.shape, lambda bb, i: (0, 0))],
        out_specs=pl.BlockSpec((1, LANE, D), lambda bb, i: (bb, i, 0)),
        out_shape=jax.ShapeDtypeStruct((B, S, D), F32),
        compiler_params=pltpu.CompilerParams(
            dimension_semantics=("arbitrary", "arbitrary"), vmem_limit_bytes=VMEM_LIMIT),
        name="out_proj_final",
    )(h, oa, ob, z, c, w, g)


def _row_tile(lq):
    for tm in (384, 256, 128):
        if lq % tm == 0:
            return tm
    raise ValueError(f"sequence of {lq} rows is not a multiple of {LANE}")


def kernel(x, meta_tokens, norm_g, w_in, b_f, lam_q1, lam_k1, lam_q2, lam_k2, subln_g, w_pool,
           pool_scale, w_out, final_g):
    B, S, D = x.shape
    depth = w_in.shape[0]
    assert S % (QSUB * TQ) == 0 and D == HEAD_W and w_in.shape[2] == HEAD_W + FOX_HEADS + TAIL_W
    Lq = LANE + S
    tm = _row_tile(Lq)
    meta = jnp.broadcast_to(meta_tokens.astype(x.dtype)[None], (B, N_META, D))
    h = jnp.concatenate([jnp.zeros((B, FRONT, D), x.dtype), meta, x], axis=1)
    w_all = _arrange_call(w_in)

    y = None
    for l in range(depth):
        lambda_init = 0.8 - 0.6 * math.exp(-0.3 * l)
        bfp = jnp.pad(b_f[l].astype(F32), (FORGET_LANE, 0))[None]
        wp = jax.scipy.linalg.block_diag(*[w_pool[l, g] for g in range(len(POOL_WINDOWS))]).astype(BF16)
        ka, qat, vt, z, cp = _proj_call(h, norm_g[l][None].astype(F32), w_all, l, bfp, wp,
                                        pool_scale[l][None].astype(F32), tm)
        oa = _attn_call(ka, qat, vt, (), diff=False, lambda_init=lambda_init)
        extra = (lam_q1[l][None].astype(F32), lam_k1[l][None].astype(F32),
                 lam_q2[l][None].astype(F32), lam_k2[l][None].astype(F32),
                 subln_g[l][:, None].astype(F32))
        ob = _attn_call(ka, qat, vt, extra, diff=True, lambda_init=lambda_init)
        wo = w_out[l].astype(BF16)
        if l + 1 < depth:
            h = _out_call(h, oa, ob, z, cp, wo, tm)
        else:
            y = _final_call(h, oa, ob, z, cp, wo, final_g[None].astype(F32))
    return y
```

```python
import functools
import math

import jax
import jax.numpy as jnp
from jax import lax
from jax.experimental import pallas as pl
from jax.experimental.pallas import tpu as pltpu

F32 = jnp.float32
BF16 = jnp.bfloat16

LANE = 128
N_META = 16
CHUNK = 64
FRONT = LANE - N_META
HEAD = 64
FOX_HEADS = 4
DIFF_HEADS = 4
N_INST = FOX_HEADS + 2 * DIFF_HEADS
FOX_W = 256
DIFF_W = 512
ATT_W = FOX_W + DIFF_W
POOL_W = 256
POOL_WINDOWS = (2, 4, 8, 16)
ALIBI_SLOPES = tuple(2.0 ** (-8.0 * (h + 1) / DIFF_HEADS) for h in range(DIFF_HEADS))
NORM_EPS = 1e-6
SUBLN_EPS = 1e-5
MASK = -1e30
M_INIT = -1e38
TQ = 256
QSUB = 4
CHAIN_KEYS = 128
NI = 4
GROUP_W = NI * HEAD
LOG2E = 1.4426950408889634
PROJ_BATCH = 2
FINAL_PARTS = 2
VMEM_LIMIT = 56 * 1024 * 1024

HEAD_W = 4 * FOX_W
TAIL_W = 4 * DIFF_W + 2 * POOL_W
TAIL_PAD = TAIL_W + LANE
W_COLS = HEAD_W + TAIL_PAD
FORGET_LANE = LANE - FOX_HEADS
C_FQ, C_FK, C_FV, C_FZ = 0, FOX_W, 2 * FOX_W, 3 * FOX_W
C_DQ, C_DK, C_DV, C_DZ = HEAD_W, HEAD_W + DIFF_W, HEAD_W + 2 * DIFF_W, HEAD_W + 3 * DIFF_W
C_PU, C_PZ, C_FG = HEAD_W + 4 * DIFF_W, HEAD_W + 4 * DIFF_W + POOL_W, HEAD_W + TAIL_W


def _split3(x):
    hi = x.astype(BF16)
    r1 = x - hi.astype(F32)
    mid = r1.astype(BF16)
    lo = (r1 - mid.astype(F32)).astype(BF16)
    return hi, mid, lo


def _arrange_kernel(head_ref, tail_ref, o_ref):
    o_ref[0, :, 0:HEAD_W] = head_ref[0].astype(BF16)
    o_ref[0, :, HEAD_W:] = pltpu.roll(tail_ref[0], TAIL_PAD - FOX_HEADS, axis=1).astype(BF16)


def _arrange_call(w_in):
    depth, d, n = w_in.shape
    head = w_in[:, :, 0:HEAD_W]
    tail = jnp.pad(w_in[:, :, HEAD_W:], ((0, 0), (0, 0), (0, TAIL_PAD - (n - HEAD_W))))
    rows = 256
    return pl.pallas_call(
        _arrange_kernel,
        grid=(depth, d // rows),
        in_specs=[pl.BlockSpec((1, rows, HEAD_W), lambda l, i: (l, i, 0)),
                  pl.BlockSpec((1, rows, TAIL_PAD), lambda l, i: (l, i, 0))],
        out_specs=pl.BlockSpec((1, rows, W_COLS), lambda l, i: (l, i, 0)),
        out_shape=jax.ShapeDtypeStruct((depth, d, W_COLS), BF16),
        compiler_params=pltpu.CompilerParams(
            dimension_semantics=("arbitrary", "arbitrary"), vmem_limit_bytes=VMEM_LIMIT),
        name="arrange_w",
    )(head, tail)


def _proj_kernel(*refs, fused):
    if fused:
        (h_ref, oa_ref, ob_ref, zp_ref, cpp_ref, wo_ref, g_ref, w_ref, bf_ref, wp_ref, ps_ref,
         ho_ref, ka_ref, qat_ref, vt_ref, z_ref, cp_ref, carry_ref, u_ref) = refs
    else:
        (h_ref, g_ref, w_ref, bf_ref, wp_ref, ps_ref,
         ka_ref, qat_ref, vt_ref, z_ref, cp_ref, carry_ref, u_ref) = refs
    i = pl.program_id(1)
    nb, tm, d = h_ref.shape
    halo = POOL_WINDOWS[-1]

    @pl.when(i == 0)
    def _():
        carry_ref[...] = jnp.zeros_like(carry_ref)
        u_ref[:, 0:halo, :] = jnp.zeros((nb, halo, POOL_W), F32)

    if fused:
        h = _residual(h_ref[0], [_mix_chunk(oa_ref, ob_ref, zp_ref, s) for s in range(tm // LANE)],
                      cpp_ref[0], wo_ref)
        ho_ref[0] = h
    else:
        h = h_ref[...].reshape(nb * tm, d)
    hn = h * lax.rsqrt(jnp.mean(h * h, axis=-1, keepdims=True) + NORM_EPS) * g_ref[...]
    hnb = hn.astype(BF16)
    fg_all = jnp.dot(hnb, w_ref[0, :, C_FG:], preferred_element_type=F32)
    proj_all = jnp.dot(hnb, w_ref[0, :, 0:C_FG], preferred_element_type=F32)

    row = i * tm + lax.broadcasted_iota(jnp.int32, (tm, LANE), 0)
    tri = (lax.broadcasted_iota(jnp.int32, (LANE, LANE), 0)
           >= lax.broadcasted_iota(jnp.int32, (LANE, LANE), 1)).astype(BF16)
    lane = lax.broadcasted_iota(jnp.int32, (LANE, LANE), 1)
    lm = lane & (HEAD - 1)
    low_half = lane < HEAD
    rloc = lax.broadcasted_iota(jnp.int32, (LANE, 1), 0)
    prow = i * tm + lax.broadcasted_iota(jnp.int32, (tm, POOL_W), 0)
    plane = lax.broadcasted_iota(jnp.int32, (tm, POOL_W), 1)
    pos1 = prow - (FRONT - 1)
    gd = POOL_W // len(POOL_WINDOWS)

    def extras(q_bias, dead):
        qh, qm, ql = (p.astype(F32) for p in _split3(q_bias))
        one_q = jnp.where((lm >= 3) & (lm < 7), 1.0, 0.0)
        eq = jnp.where(lm == 0, qh, jnp.where(lm == 1, qm, jnp.where(lm == 2, ql, one_q)))
        one_k = jnp.where(lm < 3, 1.0, 0.0)
        ek = jnp.where(lm == 3, -qh, jnp.where(lm == 4, -qm, jnp.where(lm == 5, -ql,
                                                                       jnp.where(lm == 6, dead, one_k))))
        return eq, ek

    for bb in range(nb):
        proj = proj_all[bb * tm:(bb + 1) * tm]
        fg = fg_all[bb * tm:(bb + 1) * tm]
        z_ref[bb, :, 0:FOX_W] = proj[:, C_FZ:C_FZ + FOX_W].astype(BF16)
        z_ref[bb, :, FOX_W:] = proj[:, C_DZ:C_DZ + DIFF_W].astype(BF16)
        pu = proj[:, C_PU:C_PU + POOL_W]
        pz = proj[:, C_PZ:C_PZ + POOL_W]

        logf = jnp.where(row >= FRONT, jax.nn.log_sigmoid(fg + bf_ref[...]), 0.0)
        carry = carry_ref[bb, 0:1, :]
        for s in range(tm // LANE):
            rs = slice(s * LANE, (s + 1) * LANE)
            parts = _split3(logf[rs])
            within = sum(jnp.dot(tri, p, preferred_element_type=F32) for p in parts)
            cblk = within + carry
            carry = cblk[LANE - 1:LANE, :]

            grow = i * tm + s * LANE + rloc
            dead = jnp.where(grow < FRONT, MASK, 0.0)
            for hd in range(FOX_HEADS):
                cb = LOG2E * cblk[:, FORGET_LANE + hd:FORGET_LANE + hd + 1]
                eq, ek = extras(cb, dead)
                grp = slice((hd // 2) * LANE, (hd // 2 + 1) * LANE)
                own = low_half if hd % 2 == 0 else jnp.logical_not(low_half)
                q = proj[rs, C_FQ:C_FQ + FOX_W][:, grp] * (HEAD ** -0.5 * LOG2E)
                k = proj[rs, C_FK:C_FK + FOX_W][:, grp]
                ka_ref[bb, hd, rs, :] = jnp.where(own, k, ek).astype(BF16)
                qat_ref[bb, hd, s] = jnp.where(own, q, eq).T.astype(BF16)
            for hd in range(DIFF_HEADS):
                tb = (-LOG2E * ALIBI_SLOPES[hd]) * grow.astype(F32)
                eq, ek = extras(tb, dead)
                grp = slice(hd * LANE, (hd + 1) * LANE)
                q = proj[rs, C_DQ:C_DQ + DIFF_W][:, grp] * (HEAD ** -0.5 * LOG2E)
                k = proj[rs, C_DK:C_DK + DIFF_W][:, grp]
                for c in range(2):
                    own = low_half if c == 0 else jnp.logical_not(low_half)
                    inst = FOX_HEADS + 2 * hd + c
                    ka_ref[bb, inst, rs, :] = jnp.where(own, k, ek).astype(BF16)
                    qat_ref[bb, inst, s] = jnp.where(own, q, eq).T.astype(BF16)
            for g in range(FOX_W // LANE):
                vt_ref[bb, s, g * LANE:(g + 1) * LANE, :] = (
                    proj[rs, C_FV + g * LANE:C_FV + (g + 1) * LANE].T.astype(BF16))
            for g in range(DIFF_W // LANE):
                vt_ref[bb, s, FOX_W + g * LANE:FOX_W + (g + 1) * LANE, :] = (
                    proj[rs, C_DV + g * LANE:C_DV + (g + 1) * LANE].T.astype(BF16))
        carry_ref[bb] = jnp.broadcast_to(carry, carry_ref.shape[1:])

        u_ref[bb, halo:halo + tm, :] = pu
        acc = pu
        sums = {}
        for k in range(1, halo):
            acc = acc + u_ref[bb, halo - k:halo - k + tm, :]
            if k + 1 in POOL_WINDOWS:
                sums[k + 1] = acc
        mean = None
        for gi, wdw in reversed(list(enumerate(POOL_WINDOWS))):
            cand = sums[wdw] / jnp.clip(pos1, 1, wdw).astype(F32)
            mean = cand if mean is None else jnp.where(plane < (gi + 1) * gd, cand, mean)
        pooled = mean - pu
        y = jnp.dot(pooled.astype(BF16), wp_ref[...], preferred_element_type=F32)
        cp_ref[bb] = (y * ps_ref[...] * jax.nn.silu(pz)).astype(BF16)
        u_ref[bb, 0:halo, :] = pu[tm - halo:tm, :]


def _proj_call(h, g, w_all, layer, bfp, wp, ps, tm, prev=None):
    B, Lq, D = h.shape
    fused = prev is not None
    nb = 1 if fused or B % PROJ_BATCH else PROJ_BATCH
    nt = Lq // tm
    cpt = tm // LANE
    row_spec = lambda width: pl.BlockSpec((nb, tm, width), lambda b, i: (b, i, 0))
    full = lambda a: pl.BlockSpec(a.shape, lambda b, i: (0,) * a.ndim)
    in_specs = [row_spec(D)]
    args = [h]
    out_specs, out_shape = [], []
    if fused:
        oa, ob, zp, cpp, wo = prev
        in_specs += [pl.BlockSpec((1, cpt, FOX_W, LANE), lambda b, i: (b, i, 0, 0)),
                     pl.BlockSpec((1, cpt, DIFF_W, LANE), lambda b, i: (b, i, 0, 0)),
                     row_spec(ATT_W), row_spec(POOL_W), full(wo)]
        args += [oa, ob, zp, cpp, wo]
        out_specs.append(row_spec(D))
        out_shape.append(jax.ShapeDtypeStruct((B, Lq, D), F32))
    in_specs += [full(g), pl.BlockSpec((1, D, W_COLS), lambda b, i: (layer, 0, 0)),
                 full(bfp), full(wp), full(ps)]
    args += [g, w_all, bfp, wp, ps]
    out_specs += [pl.BlockSpec((nb, N_INST, tm, LANE), lambda b, i: (b, 0, i, 0)),
                  pl.BlockSpec((nb, N_INST, cpt, LANE, LANE), lambda b, i: (b, 0, i, 0, 0)),
                  pl.BlockSpec((nb, cpt, ATT_W, LANE), lambda b, i: (b, i, 0, 0)),
                  row_spec(ATT_W), row_spec(POOL_W)]
    out_shape += [jax.ShapeDtypeStruct((B, N_INST, Lq, LANE), BF16),
                  jax.ShapeDtypeStruct((B, N_INST, Lq // LANE, LANE, LANE), BF16),
                  jax.ShapeDtypeStruct((B, Lq // LANE, ATT_W, LANE), BF16),
                  jax.ShapeDtypeStruct((B, Lq, ATT_W), BF16),
                  jax.ShapeDtypeStruct((B, Lq, POOL_W), BF16)]
    return pl.pallas_call(
        functools.partial(_proj_kernel, fused=fused),
        grid=(B // nb, nt),
        in_specs=in_specs,
        out_specs=out_specs,
        out_shape=out_shape,
        scratch_shapes=[pltpu.VMEM((nb, 8, LANE), F32),
                        pltpu.VMEM((nb, POOL_WINDOWS[-1] + tm, POOL_W), F32)],
        compiler_params=pltpu.CompilerParams(
            dimension_semantics=("arbitrary", "arbitrary"), vmem_limit_bytes=VMEM_LIMIT),
        name="layer" if fused else "proj",
    )(*args)


def _attn_kernel(*refs, diff, lambda_init):
    if diff:
        ka_ref, qat_ref, vt_ref, lq1_ref, lk1_ref, lq2_ref, lk2_ref, sg_ref, ot_ref, dm_ref = refs
    else:
        ka_ref, qat_ref, vt_ref, ot_ref, dm_ref = refs
    lq = ka_ref.shape[2]
    n_sblocks = (lq - LANE) // (QSUB * TQ)
    pid = pl.program_id(1)
    dv = LANE if diff else HEAD
    kc = TQ // LANE

    kk = lax.broadcasted_iota(jnp.int32, (TQ, TQ), 0)
    qq = lax.broadcasted_iota(jnp.int32, (TQ, TQ), 1)
    if diff:
        for hh in range(NI // 2):
            slope = jnp.where(pid == 0, ALIBI_SLOPES[hh], ALIBI_SLOPES[NI // 2 + hh]).astype(F32)
            fix = jnp.where(kk > qq, (2.0 * LOG2E * slope) * (qq - kk).astype(F32), 0.0)
            dm_ref[hh] = jnp.where(kk // CHUNK > qq // CHUNK, MASK, fix)
        lam = (jnp.exp(jnp.sum(lq1_ref[...] * lk1_ref[...], axis=-1, keepdims=True))
               - jnp.exp(jnp.sum(lq2_ref[...] * lk2_ref[...], axis=-1, keepdims=True))
               + lambda_init)
    else:
        dm_ref[0] = jnp.where(kk > qq, MASK, 0.0)

    def vrows(inst):
        return slice(LANE * (inst // 2), LANE * (inst // 2 + 1))

    def dm(inst, rows=slice(None), cols=slice(None)):
        return dm_ref[inst // 2 if diff else 0, rows, cols]

    def scores(inst, qat, key_start, n_key_chunks):
        ka = ka_ref[0, inst, pl.ds(key_start, n_key_chunks * LANE), :]
        return jnp.dot(ka, qat, preferred_element_type=F32)

    def partial(inst, s, first_chunk):
        m = jnp.max(s, axis=0, keepdims=True)
        p = jnp.exp2(s - m)
        l = jnp.sum(p, axis=0, keepdims=True)
        pb = p.astype(BF16)
        pv = None
        for cch in range(s.shape[0] // LANE):
            vt = vt_ref[0, first_chunk + cch, vrows(inst), :]
            d = jnp.dot(vt, pb[cch * LANE:(cch + 1) * LANE], preferred_element_type=F32)
            pv = d if pv is None else pv + d
        if not diff:
            pv = pv[HEAD * (inst % 2):HEAD * (inst % 2 + 1)]
        return m, l, pv

    def merge(state, parts):
        m, l, acc = state
        m_new = m
        for mp, _, _ in parts:
            m_new = jnp.maximum(m_new, mp)
        w = jnp.exp2(m - m_new)
        l = w * l
        acc = w * acc
        for mp, lp, pvp in parts:
            wp = jnp.exp2(mp - m_new)
            l = l + wp * lp
            acc = acc + wp * pvp
        return m_new, l, acc

    def init_state(tq):
        return (jnp.full((1, tq), M_INIT, F32), jnp.zeros((1, tq), F32), jnp.zeros((dv, tq), F32))

    def finish(states, first_chunk, n_q_chunks):
        if diff:
            heads = []
            for hh in range(NI // 2):
                (_, l0, a0), (_, l1, a1) = states[2 * hh], states[2 * hh + 1]
                o = a0 / l0 - lam * (a1 / l1)
                o = o * lax.rsqrt(jnp.mean(o * o, axis=0, keepdims=True) + SUBLN_EPS) * sg_ref[...]
                heads.append(o * (1.0 - lambda_init))
        else:
            heads = [acc / l for (_, l, acc) in states]
        o = jnp.concatenate(heads, axis=0)
        for cch in range(n_q_chunks):
            ot_ref[0, first_chunk + cch] = o[:, cch * LANE:(cch + 1) * LANE]

    lead = []
    for inst in range(NI):
        s = scores(inst, qat_ref[0, inst, 0], 0, 1) + dm(inst, slice(0, LANE), slice(0, LANE))
        lead.append(merge(init_state(LANE), [partial(inst, s, 0)]))
    finish(lead, 0, 1)

    pairs = [(inst, qs) for inst in range(NI) for qs in range(QSUB)]

    def superblock(i, _):
        qats = {}
        for inst, qs in pairs:
            c0 = 1 + kc * (QSUB * i + qs)
            qats[inst, qs] = jnp.concatenate([qat_ref[0, inst, c0 + c] for c in range(kc)], axis=1)

        def kvstep(j, carry):
            k_start = pl.multiple_of(LANE + j * (QSUB * TQ), LANE)
            out = []
            for n, (inst, qs) in enumerate(pairs):
                s = scores(inst, qats[inst, qs], k_start, QSUB * kc)
                parts = [partial(inst, s[u * CHAIN_KEYS:(u + 1) * CHAIN_KEYS],
                                 1 + kc * QSUB * j + u * (CHAIN_KEYS // LANE))
                         for u in range(QSUB * TQ // CHAIN_KEYS)]
                out.extend(merge(carry[3 * n:3 * n + 3], parts))
            return tuple(out)

        carry0 = tuple(x for _ in pairs for x in init_state(TQ))
        carry = lax.fori_loop(0, i, kvstep, carry0)

        d_start = pl.multiple_of(LANE + i * (QSUB * TQ), LANE)
        for qs in range(QSUB):
            states = []
            for inst in range(NI):
                n = pairs.index((inst, qs))
                parts = [partial(inst, scores(inst, qats[inst, qs], 0, 1), 0)]
                s = scores(inst, qats[inst, qs], d_start, (qs + 1) * kc)
                for u in range(qs + 1):
                    su = s[u * TQ:(u + 1) * TQ]
                    if u == qs:
                        su = su + dm(inst)
                    parts.append(partial(inst, su, 1 + kc * (QSUB * i + u)))
                states.append(merge(carry[3 * n:3 * n + 3], parts))
            finish(states, 1 + kc * (QSUB * i + qs), kc)
        return 0

    lax.fori_loop(0, n_sblocks, superblock, 0)


def _attn_call(ka, qat, vt, extra, *, diff, lambda_init):
    B, _, Lq, _ = ka.shape
    nc = Lq // LANE
    n_groups = 2 * DIFF_HEADS // NI if diff else FOX_HEADS // NI
    first = FOX_HEADS // NI if diff else 0
    in_specs = [pl.BlockSpec((1, NI, Lq, LANE), lambda b, g: (b, first + g, 0, 0)),
                pl.BlockSpec((1, NI, nc, LANE, LANE), lambda b, g: (b, first + g, 0, 0, 0)),
                pl.BlockSpec((1, nc, GROUP_W, LANE), lambda b, g: (b, 0, first + g, 0))]
    args = [ka, qat, vt]
    for a in extra:
        in_specs.append(pl.BlockSpec(a.shape, lambda b, g, nd=a.ndim: (0,) * nd))
        args.append(a)
    return pl.pallas_call(
        functools.partial(_attn_kernel, diff=diff, lambda_init=lambda_init),
        grid=(B, n_groups),
        in_specs=in_specs,
        out_specs=pl.BlockSpec((1, nc, GROUP_W, LANE), lambda b, g: (b, 0, g, 0)),
        out_shape=jax.ShapeDtypeStruct((B, nc, n_groups * GROUP_W, LANE), F32),
        scratch_shapes=[pltpu.VMEM((NI // 2 if diff else 1, TQ, TQ), F32)],
        compiler_params=pltpu.CompilerParams(
            dimension_semantics=("arbitrary", "arbitrary"), vmem_limit_bytes=VMEM_LIMIT),
        name="diff_attn" if diff else "fox_attn",
    )(*args)


def _mix_chunk(oa_ref, ob_ref, z_ref, s):
    pieces = [oa_ref[0, s, g * LANE:(g + 1) * LANE, :].T for g in range(FOX_W // LANE)]
    pieces += [ob_ref[0, s, g * LANE:(g + 1) * LANE, :].T for g in range(DIFF_W // LANE)]
    att = jnp.concatenate(pieces, axis=1)
    gate = jax.nn.silu(z_ref[0, s * LANE:(s + 1) * LANE, :].astype(F32))
    return (att * gate).astype(BF16)


def _residual(h, mix_chunks, c, w_ref):
    mix = mix_chunks[0] if len(mix_chunks) == 1 else jnp.concatenate(mix_chunks, axis=0)
    return (h + jnp.dot(mix, w_ref[0:ATT_W, :], preferred_element_type=F32)
            + jnp.dot(c, w_ref[ATT_W:, :], preferred_element_type=F32))


def _final_kernel(*refs):
    ins, (w_ref, g_ref, o_ref) = refs[:-3], refs[-3:]
    parts = [ins[5 * p:5 * p + 5] for p in range(FINAL_PARTS)]
    h = jnp.concatenate([p[0][0] for p in parts], axis=0)
    c = jnp.concatenate([p[4][0] for p in parts], axis=0)
    hn = _residual(h, [_mix_chunk(p[1], p[2], p[3], 0) for p in parts], c, w_ref)
    o_ref[0] = hn * lax.rsqrt(jnp.mean(hn * hn, axis=-1, keepdims=True) + NORM_EPS) * g_ref[...]


def _final_call(h, oa, ob, z, c, w, g):
    B, Lq, D = h.shape
    S = Lq - LANE
    in_specs, args = [], []
    for p in range(FINAL_PARTS):
        chunk = lambda bb, i, p=p: (bb, FINAL_PARTS * i + p + 1)
        row_spec = lambda width, chunk=chunk: pl.BlockSpec((1, LANE, width), lambda bb, i: (*chunk(bb, i), 0))
        in_specs += [row_spec(D),
                     pl.BlockSpec((1, 1, FOX_W, LANE), lambda bb, i, chunk=chunk: (*chunk(bb, i), 0, 0)),
                     pl.BlockSpec((1, 1, DIFF_W, LANE), lambda bb, i, chunk=chunk: (*chunk(bb, i), 0, 0)),
                     row_spec(ATT_W), row_spec(POOL_W)]
        args += [h, oa, ob, z, c]
    in_specs += [pl.BlockSpec(w.shape, lambda bb, i: (0, 0)), pl.BlockSpec(g.shape, lambda bb, i: (0, 0))]
    return pl.pallas_call(
        _final_kernel,
        grid=(B, S // (FINAL_PARTS * LANE)),
        in_specs=in_specs,
        out_specs=pl.BlockSpec((1, FINAL_PARTS * LANE, D), lambda bb, i: (bb, i, 0)),
        out_shape=jax.ShapeDtypeStruct((B, S, D), F32),
        compiler_params=pltpu.CompilerParams(
            dimension_semantics=("arbitrary", "arbitrary"), vmem_limit_bytes=VMEM_LIMIT),
        name="out_proj_final",
    )(*args, w, g)


def _row_tile(lq):
    for tm in (384, 256, 128):
        if lq % tm == 0:
            return tm
    raise ValueError(f"sequence of {lq} rows is not a multiple of {LANE}")


def kernel(x, meta_tokens, norm_g, w_in, b_f, lam_q1, lam_k1, lam_q2, lam_k2, subln_g, w_pool,
           pool_scale, w_out, final_g):
    B, S, D = x.shape
    depth = w_in.shape[0]
    assert S % (QSUB * TQ) == 0 and D == HEAD_W and w_in.shape[2] == HEAD_W + FOX_HEADS + TAIL_W
    Lq = LANE + S
    tm = _row_tile(Lq)
    meta = jnp.broadcast_to(meta_tokens.astype(x.dtype)[None], (B, N_META, D))
    h = jnp.concatenate([jnp.zeros((B, FRONT, D), x.dtype), meta, x], axis=1)
    w_all = _arrange_call(w_in)

    prev = None
    for l in range(depth):
        lambda_init = 0.8 - 0.6 * math.exp(-0.3 * l)
        bfp = jnp.pad(b_f[l].astype(F32), (FORGET_LANE, 0))[None]
        wp = jax.scipy.linalg.block_diag(*[w_pool[l, g] for g in range(len(POOL_WINDOWS))]).astype(BF16)
        outs = _proj_call(h, norm_g[l][None].astype(F32), w_all, l, bfp, wp,
                          pool_scale[l][None].astype(F32), tm, prev)
        if prev is not None:
            h, *outs = outs
        ka, qat, vt, z, cp = outs
        oa = _attn_call(ka, qat, vt, (), diff=False, lambda_init=lambda_init)
        extra = (lam_q1[l][None].astype(F32), lam_k1[l][None].astype(F32),
                 lam_q2[l][None].astype(F32), lam_k2[l][None].astype(F32),
                 subln_g[l][:, None].astype(F32))
        ob = _attn_call(ka, qat, vt, extra, diff=True, lambda_init=lambda_init)
        prev = (oa, ob, z, cp, w_out[l].astype(BF16))
    return _final_call(h, *prev, final_g[None].astype(F32))
```

```python
import functools
import math

import jax
import jax.numpy as jnp
from jax import lax
from jax.experimental import pallas as pl
from jax.experimental.pallas import tpu as pltpu

F32 = jnp.float32
BF16 = jnp.bfloat16

LANE = 128
N_META = 16
CHUNK = 64
FRONT = LANE - N_META
HEAD = 64
FOX_HEADS = 4
DIFF_HEADS = 4
N_INST = FOX_HEADS + 2 * DIFF_HEADS
FOX_W = 256
DIFF_W = 512
ATT_W = FOX_W + DIFF_W
POOL_W = 256
POOL_WINDOWS = (2, 4, 8, 16)
ALIBI_SLOPES = tuple(2.0 ** (-8.0 * (h + 1) / DIFF_HEADS) for h in range(DIFF_HEADS))
NORM_EPS = 1e-6
SUBLN_EPS = 1e-5
MASK = -1e30
M_INIT = -1e38
TQ = 256
QSUB = 4
CHAIN_KEYS = 128
NI = 4
GROUP_W = NI * HEAD
LOG2E = 1.4426950408889634
PROJ_BATCH = 2
FINAL_PARTS = 2
VMEM_LIMIT = 56 * 1024 * 1024

HEAD_W = 4 * FOX_W
TAIL_W = 4 * DIFF_W + 2 * POOL_W
TAIL_PAD = TAIL_W + LANE
W_COLS = HEAD_W + TAIL_PAD
FORGET_LANE = LANE - FOX_HEADS
C_FQ, C_FK, C_FV, C_FZ = 0, FOX_W, 2 * FOX_W, 3 * FOX_W
C_DQ, C_DK, C_DV, C_DZ = HEAD_W, HEAD_W + DIFF_W, HEAD_W + 2 * DIFF_W, HEAD_W + 3 * DIFF_W
C_PU, C_PZ, C_FG = HEAD_W + 4 * DIFF_W, HEAD_W + 4 * DIFF_W + POOL_W, HEAD_W + TAIL_W


def _split3(x):
    hi = x.astype(BF16)
    r1 = x - hi.astype(F32)
    mid = r1.astype(BF16)
    lo = (r1 - mid.astype(F32)).astype(BF16)
    return hi, mid, lo


def _arrange_kernel(head_ref, tail_ref, o_ref):
    o_ref[0, :, 0:HEAD_W] = head_ref[0].astype(BF16)
    o_ref[0, :, HEAD_W:] = pltpu.roll(tail_ref[0], TAIL_PAD - FOX_HEADS, axis=1).astype(BF16)


def _arrange_call(w_in):
    depth, d, n = w_in.shape
    head = w_in[:, :, 0:HEAD_W]
    tail = jnp.pad(w_in[:, :, HEAD_W:], ((0, 0), (0, 0), (0, TAIL_PAD - (n - HEAD_W))))
    rows = 256
    return pl.pallas_call(
        _arrange_kernel,
        grid=(depth, d // rows),
        in_specs=[pl.BlockSpec((1, rows, HEAD_W), lambda l, i: (l, i, 0)),
                  pl.BlockSpec((1, rows, TAIL_PAD), lambda l, i: (l, i, 0))],
        out_specs=pl.BlockSpec((1, rows, W_COLS), lambda l, i: (l, i, 0)),
        out_shape=jax.ShapeDtypeStruct((depth, d, W_COLS), BF16),
        compiler_params=pltpu.CompilerParams(
            dimension_semantics=("arbitrary", "arbitrary"), vmem_limit_bytes=VMEM_LIMIT),
        name="arrange_w",
    )(head, tail)


def _proj_kernel(*refs, fused, x_chunks):
    i = pl.program_id(1)
    if x_chunks:
        x_refs, lead_ref, refs = refs[:x_chunks], refs[x_chunks], refs[x_chunks + 1:]
        nb, _, d = x_refs[0].shape
        tm = x_chunks * LANE
        first = jnp.where(i == 0, jnp.broadcast_to(lead_ref[...], (nb, LANE, d)), x_refs[0][...])
        h_in = jnp.concatenate([first] + [r[...] for r in x_refs[1:]], axis=1).reshape(nb * tm, d)
    else:
        h_ref, refs = refs[0], refs[1:]
        nb, tm, d = h_ref.shape
        h_in = h_ref[...].reshape(nb * tm, d)
    if fused:
        (oa_ref, ob_ref, zp_ref, cpp_ref, wo_ref, g_ref, w_ref, bf_ref, wp_ref, ps_ref,
         ho_ref, ka_ref, qat_ref, vt_ref, z_ref, cp_ref, carry_ref, u_ref) = refs
    else:
        (g_ref, w_ref, bf_ref, wp_ref, ps_ref,
         ka_ref, qat_ref, vt_ref, z_ref, cp_ref, carry_ref, u_ref) = refs
    halo = POOL_WINDOWS[-1]

    @pl.when(i == 0)
    def _():
        carry_ref[...] = jnp.zeros_like(carry_ref)
        u_ref[:, 0:halo, :] = jnp.zeros((nb, halo, POOL_W), F32)

    if fused:
        chunks = [_mix_chunk(oa_ref, ob_ref, zp_ref, s, bb) for bb in range(nb) for s in range(tm // LANE)]
        h = _residual(h_in, chunks, cpp_ref[...].reshape(nb * tm, POOL_W), wo_ref)
        ho_ref[...] = h.reshape(nb, tm, d)
    else:
        h = h_in
    hn = h * lax.rsqrt(jnp.mean(h * h, axis=-1, keepdims=True) + NORM_EPS) * g_ref[...]
    hnb = hn.astype(BF16)
    fg_all = jnp.dot(hnb, w_ref[0, :, C_FG:], preferred_element_type=F32)
    proj_all = jnp.dot(hnb, w_ref[0, :, 0:C_FG], preferred_element_type=F32)

    row = i * tm + lax.broadcasted_iota(jnp.int32, (tm, LANE), 0)
    tri = (lax.broadcasted_iota(jnp.int32, (LANE, LANE), 0)
           >= lax.broadcasted_iota(jnp.int32, (LANE, LANE), 1)).astype(BF16)
    lane = lax.broadcasted_iota(jnp.int32, (LANE, LANE), 1)
    lm = lane & (HEAD - 1)
    low_half = lane < HEAD
    rloc = lax.broadcasted_iota(jnp.int32, (LANE, 1), 0)
    prow = i * tm + lax.broadcasted_iota(jnp.int32, (tm, POOL_W), 0)
    plane = lax.broadcasted_iota(jnp.int32, (tm, POOL_W), 1)
    pos1 = prow - (FRONT - 1)
    gd = POOL_W // len(POOL_WINDOWS)

    def extras(q_bias, dead):
        qh, qm, ql = (p.astype(F32) for p in _split3(q_bias))
        one_q = jnp.where((lm >= 3) & (lm < 7), 1.0, 0.0)
        eq = jnp.where(lm == 0, qh, jnp.where(lm == 1, qm, jnp.where(lm == 2, ql, one_q)))
        one_k = jnp.where(lm < 3, 1.0, 0.0)
        ek = jnp.where(lm == 3, -qh, jnp.where(lm == 4, -qm, jnp.where(lm == 5, -ql,
                                                                       jnp.where(lm == 6, dead, one_k))))
        return eq, ek

    for bb in range(nb):
        proj = proj_all[bb * tm:(bb + 1) * tm]
        fg = fg_all[bb * tm:(bb + 1) * tm]
        z_ref[bb, :, 0:FOX_W] = proj[:, C_FZ:C_FZ + FOX_W].astype(BF16)
        z_ref[bb, :, FOX_W:] = proj[:, C_DZ:C_DZ + DIFF_W].astype(BF16)
        pu = proj[:, C_PU:C_PU + POOL_W]
        pz = proj[:, C_PZ:C_PZ + POOL_W]

        logf = jnp.where(row >= FRONT, jax.nn.log_sigmoid(fg + bf_ref[...]), 0.0)
        carry = carry_ref[bb, 0:1, :]
        for s in range(tm // LANE):
            rs = slice(s * LANE, (s + 1) * LANE)
            parts = _split3(logf[rs])
            within = sum(jnp.dot(tri, p, preferred_element_type=F32) for p in parts)
            cblk = within + carry
            carry = cblk[LANE - 1:LANE, :]

            grow = i * tm + s * LANE + rloc
            dead = jnp.where(grow < FRONT, MASK, 0.0)
            for hd in range(FOX_HEADS):
                cb = LOG2E * cblk[:, FORGET_LANE + hd:FORGET_LANE + hd + 1]
                eq, ek = extras(cb, dead)
                grp = slice((hd // 2) * LANE, (hd // 2 + 1) * LANE)
                own = low_half if hd % 2 == 0 else jnp.logical_not(low_half)
                q = proj[rs, C_FQ:C_FQ + FOX_W][:, grp] * (HEAD ** -0.5 * LOG2E)
                k = proj[rs, C_FK:C_FK + FOX_W][:, grp]
                ka_ref[bb, hd, rs, :] = jnp.where(own, k, ek).astype(BF16)
                qat_ref[bb, hd, s] = jnp.where(own, q, eq).T.astype(BF16)
            for hd in range(DIFF_HEADS):
                tb = (-LOG2E * ALIBI_SLOPES[hd]) * grow.astype(F32)
                eq, ek = extras(tb, dead)
                grp = slice(hd * LANE, (hd + 1) * LANE)
                q = proj[rs, C_DQ:C_DQ + DIFF_W][:, grp] * (HEAD ** -0.5 * LOG2E)
                k = proj[rs, C_DK:C_DK + DIFF_W][:, grp]
                for c in range(2):
                    own = low_half if c == 0 else jnp.logical_not(low_half)
                    inst = FOX_HEADS + 2 * hd + c
                    ka_ref[bb, inst, rs, :] = jnp.where(own, k, ek).astype(BF16)
                    qat_ref[bb, inst, s] = jnp.where(own, q, eq).T.astype(BF16)
            for g in range(FOX_W // LANE):
                vt_ref[bb, s, g * LANE:(g + 1) * LANE, :] = (
                    proj[rs, C_FV + g * LANE:C_FV + (g + 1) * LANE].T.astype(BF16))
            for g in range(DIFF_W // LANE):
                vt_ref[bb, s, FOX_W + g * LANE:FOX_W + (g + 1) * LANE, :] = (
                    proj[rs, C_DV + g * LANE:C_DV + (g + 1) * LANE].T.astype(BF16))
        carry_ref[bb] = jnp.broadcast_to(carry, carry_ref.shape[1:])

        u_ref[bb, halo:halo + tm, :] = pu
        acc = pu
        sums = {}
        for k in range(1, halo):
            acc = acc + u_ref[bb, halo - k:halo - k + tm, :]
            if k + 1 in POOL_WINDOWS:
                sums[k + 1] = acc
        mean = None
        for gi, wdw in reversed(list(enumerate(POOL_WINDOWS))):
            cand = sums[wdw] / jnp.clip(pos1, 1, wdw).astype(F32)
            mean = cand if mean is None else jnp.where(plane < (gi + 1) * gd, cand, mean)
        pooled = mean - pu
        y = jnp.dot(pooled.astype(BF16), wp_ref[...], preferred_element_type=F32)
        cp_ref[bb] = (y * ps_ref[...] * jax.nn.silu(pz)).astype(BF16)
        u_ref[bb, 0:halo, :] = pu[tm - halo:tm, :]


def _proj_call(h, g, w_all, layer, bfp, wp, ps, tm, prev=None, lead=None):
    B, rows_in, D = h.shape
    from_x = lead is not None
    Lq = rows_in + LANE if from_x else rows_in
    fused = prev is not None
    nb = 1 if B % PROJ_BATCH else PROJ_BATCH
    nt = Lq // tm
    cpt = tm // LANE
    row_spec = lambda width: pl.BlockSpec((nb, tm, width), lambda b, i: (b, i, 0))
    full = lambda a: pl.BlockSpec(a.shape, lambda b, i: (0,) * a.ndim)
    if from_x:
        in_specs = [pl.BlockSpec((nb, LANE, D), lambda b, i, j=j: (b, jnp.maximum(cpt * i + j - 1, 0), 0))
                    for j in range(cpt)] + [full(lead)]
        args = [h] * cpt + [lead]
    else:
        in_specs = [row_spec(D)]
        args = [h]
    out_specs, out_shape = [], []
    if fused:
        oa, ob, zp, cpp, wo = prev
        in_specs += [pl.BlockSpec((nb, cpt, FOX_W, LANE), lambda b, i: (b, i, 0, 0)),
                     pl.BlockSpec((nb, cpt, DIFF_W, LANE), lambda b, i: (b, i, 0, 0)),
                     row_spec(ATT_W), row_spec(POOL_W), full(wo)]
        args += [oa, ob, zp, cpp, wo]
        out_specs.append(row_spec(D))
        out_shape.append(jax.ShapeDtypeStruct((B, Lq, D), F32))
    in_specs += [full(g), pl.BlockSpec((1, D, W_COLS), lambda b, i: (layer, 0, 0)),
                 full(bfp), full(wp), full(ps)]
    args += [g, w_all, bfp, wp, ps]
    out_specs += [pl.BlockSpec((nb, N_INST, tm, LANE), lambda b, i: (b, 0, i, 0)),
                  pl.BlockSpec((nb, N_INST, cpt, LANE, LANE), lambda b, i: (b, 0, i, 0, 0)),
                  pl.BlockSpec((nb, cpt, ATT_W, LANE), lambda b, i: (b, i, 0, 0)),
                  row_spec(ATT_W), row_spec(POOL_W)]
    out_shape += [jax.ShapeDtypeStruct((B, N_INST, Lq, LANE), BF16),
                  jax.ShapeDtypeStruct((B, N_INST, Lq // LANE, LANE, LANE), BF16),
                  jax.ShapeDtypeStruct((B, Lq // LANE, ATT_W, LANE), BF16),
                  jax.ShapeDtypeStruct((B, Lq, ATT_W), BF16),
                  jax.ShapeDtypeStruct((B, Lq, POOL_W), BF16)]
    return pl.pallas_call(
        functools.partial(_proj_kernel, fused=fused, x_chunks=cpt if from_x else 0),
        grid=(B // nb, nt),
        in_specs=in_specs,
        out_specs=out_specs,
        out_shape=out_shape,
        scratch_shapes=[pltpu.VMEM((nb, 8, LANE), F32),
                        pltpu.VMEM((nb, POOL_WINDOWS[-1] + tm, POOL_W), F32)],
        compiler_params=pltpu.CompilerParams(
            dimension_semantics=("arbitrary", "arbitrary"), vmem_limit_bytes=VMEM_LIMIT),
        name="layer" if fused else "proj",
    )(*args)


def _attn_kernel(*refs, diff, lambda_init):
    if diff:
        ka_ref, qat_ref, vt_ref, lq1_ref, lk1_ref, lq2_ref, lk2_ref, sg_ref, ot_ref, dm_ref = refs
    else:
        ka_ref, qat_ref, vt_ref, ot_ref, dm_ref = refs
    lq = ka_ref.shape[2]
    n_sblocks = (lq - LANE) // (QSUB * TQ)
    pid = pl.program_id(1)
    dv = LANE if diff else HEAD
    kc = TQ // LANE

    kk = lax.broadcasted_iota(jnp.int32, (TQ, TQ), 0)
    qq = lax.broadcasted_iota(jnp.int32, (TQ, TQ), 1)
    if diff:
        for hh in range(NI // 2):
            slope = jnp.where(pid == 0, ALIBI_SLOPES[hh], ALIBI_SLOPES[NI // 2 + hh]).astype(F32)
            fix = jnp.where(kk > qq, (2.0 * LOG2E * slope) * (qq - kk).astype(F32), 0.0)
            dm_ref[hh] = jnp.where(kk // CHUNK > qq // CHUNK, MASK, fix)
        lam = (jnp.exp(jnp.sum(lq1_ref[...] * lk1_ref[...], axis=-1, keepdims=True))
               - jnp.exp(jnp.sum(lq2_ref[...] * lk2_ref[...], axis=-1, keepdims=True))
               + lambda_init)
    else:
        dm_ref[0] = jnp.where(kk > qq, MASK, 0.0)

    def vrows(inst):
        return slice(LANE * (inst // 2), LANE * (inst // 2 + 1))

    def dm(inst, rows=slice(None), cols=slice(None)):
        return dm_ref[inst // 2 if diff else 0, rows, cols]

    def scores(inst, qat, key_start, n_key_chunks):
        ka = ka_ref[0, inst, pl.ds(key_start, n_key_chunks * LANE), :]
        return jnp.dot(ka, qat, preferred_element_type=F32)

    def partial(inst, s, first_chunk):
        m = jnp.max(s, axis=0, keepdims=True)
        p = jnp.exp2(s - m)
        l = jnp.sum(p, axis=0, keepdims=True)
        pb = p.astype(BF16)
        pv = None
        for cch in range(s.shape[0] // LANE):
            vt = vt_ref[0, first_chunk + cch, vrows(inst), :]
            d = jnp.dot(vt, pb[cch * LANE:(cch + 1) * LANE], preferred_element_type=F32)
            pv = d if pv is None else pv + d
        if not diff:
            pv = pv[HEAD * (inst % 2):HEAD * (inst % 2 + 1)]
        return m, l, pv

    def merge(state, parts):
        m, l, acc = state
        m_new = m
        for mp, _, _ in parts:
            m_new = jnp.maximum(m_new, mp)
        w = jnp.exp2(m - m_new)
        l = w * l
        acc = w * acc
        for mp, lp, pvp in parts:
            wp = jnp.exp2(mp - m_new)
            l = l + wp * lp
            acc = acc + wp * pvp
        return m_new, l, acc

    def init_state(tq):
        return (jnp.full((1, tq), M_INIT, F32), jnp.zeros((1, tq), F32), jnp.zeros((dv, tq), F32))

    def finish(states, first_chunk, n_q_chunks):
        if diff:
            heads = []
            for hh in range(NI // 2):
                (_, l0, a0), (_, l1, a1) = states[2 * hh], states[2 * hh + 1]
                o = a0 / l0 - lam * (a1 / l1)
                o = o * lax.rsqrt(jnp.mean(o * o, axis=0, keepdims=True) + SUBLN_EPS) * sg_ref[...]
                heads.append(o * (1.0 - lambda_init))
        else:
            heads = [acc / l for (_, l, acc) in states]
        o = jnp.concatenate(heads, axis=0)
        for cch in range(n_q_chunks):
            ot_ref[0, first_chunk + cch] = o[:, cch * LANE:(cch + 1) * LANE]

    lead = []
    for inst in range(NI):
        s = scores(inst, qat_ref[0, inst, 0], 0, 1) + dm(inst, slice(0, LANE), slice(0, LANE))
        lead.append(merge(init_state(LANE), [partial(inst, s, 0)]))
    finish(lead, 0, 1)

    pairs = [(inst, qs) for inst in range(NI) for qs in range(QSUB)]

    def superblock(i, _):
        qats = {}
        for inst, qs in pairs:
            c0 = 1 + kc * (QSUB * i + qs)
            qats[inst, qs] = jnp.concatenate([qat_ref[0, inst, c0 + c] for c in range(kc)], axis=1)

        def kvstep(j, carry):
            k_start = pl.multiple_of(LANE + j * (QSUB * TQ), LANE)
            out = []
            for n, (inst, qs) in enumerate(pairs):
                s = scores(inst, qats[inst, qs], k_start, QSUB * kc)
                parts = [partial(inst, s[u * CHAIN_KEYS:(u + 1) * CHAIN_KEYS],
                                 1 + kc * QSUB * j + u * (CHAIN_KEYS // LANE))
                         for u in range(QSUB * TQ // CHAIN_KEYS)]
                out.extend(merge(carry[3 * n:3 * n + 3], parts))
            return tuple(out)

        carry0 = tuple(x for _ in pairs for x in init_state(TQ))
        carry = lax.fori_loop(0, i, kvstep, carry0)

        d_start = pl.multiple_of(LANE + i * (QSUB * TQ), LANE)
        for qs in range(QSUB):
            states = []
            for inst in range(NI):
                n = pairs.index((inst, qs))
                parts = [partial(inst, scores(inst, qats[inst, qs], 0, 1), 0)]
                s = scores(inst, qats[inst, qs], d_start, (qs + 1) * kc)
                for u in range(qs + 1):
                    su = s[u * TQ:(u + 1) * TQ]
                    if u == qs:
                        su = su + dm(inst)
                    parts.append(partial(inst, su, 1 + kc * (QSUB * i + u)))
                states.append(merge(carry[3 * n:3 * n + 3], parts))
            finish(states, 1 + kc * (QSUB * i + qs), kc)
        return 0

    lax.fori_loop(0, n_sblocks, superblock, 0)


def _attn_call(ka, qat, vt, extra, *, diff, lambda_init):
    B, _, Lq, _ = ka.shape
    nc = Lq // LANE
    n_groups = 2 * DIFF_HEADS // NI if diff else FOX_HEADS // NI
    first = FOX_HEADS // NI if diff else 0
    in_specs = [pl.BlockSpec((1, NI, Lq, LANE), lambda b, g: (b, first + g, 0, 0)),
                pl.BlockSpec((1, NI, nc, LANE, LANE), lambda b, g: (b, first + g, 0, 0, 0)),
                pl.BlockSpec((1, nc, GROUP_W, LANE), lambda b, g: (b, 0, first + g, 0))]
    args = [ka, qat, vt]
    for a in extra:
        in_specs.append(pl.BlockSpec(a.shape, lambda b, g, nd=a.ndim: (0,) * nd))
        args.append(a)
    return pl.pallas_call(
        functools.partial(_attn_kernel, diff=diff, lambda_init=lambda_init),
        grid=(B, n_groups),
        in_specs=in_specs,
        out_specs=pl.BlockSpec((1, nc, GROUP_W, LANE), lambda b, g: (b, 0, g, 0)),
        out_shape=jax.ShapeDtypeStruct((B, nc, n_groups * GROUP_W, LANE), F32),
        scratch_shapes=[pltpu.VMEM((NI // 2 if diff else 1, TQ, TQ), F32)],
        compiler_params=pltpu.CompilerParams(
            dimension_semantics=("arbitrary", "arbitrary"), vmem_limit_bytes=VMEM_LIMIT),
        name="diff_attn" if diff else "fox_attn",
    )(*args)


def _mix_chunk(oa_ref, ob_ref, z_ref, s, bb=0):
    pieces = [oa_ref[bb, s, g * LANE:(g + 1) * LANE, :].T for g in range(FOX_W // LANE)]
    pieces += [ob_ref[bb, s, g * LANE:(g + 1) * LANE, :].T for g in range(DIFF_W // LANE)]
    att = jnp.concatenate(pieces, axis=1)
    gate = jax.nn.silu(z_ref[bb, s * LANE:(s + 1) * LANE, :].astype(F32))
    return (att * gate).astype(BF16)


def _residual(h, mix_chunks, c, w_ref):
    mix = mix_chunks[0] if len(mix_chunks) == 1 else jnp.concatenate(mix_chunks, axis=0)
    return (h + jnp.dot(mix, w_ref[0:ATT_W, :], preferred_element_type=F32)
            + jnp.dot(c, w_ref[ATT_W:, :], preferred_element_type=F32))


def _final_kernel(*refs):
    ins, (w_ref, g_ref, o_ref) = refs[:-3], refs[-3:]
    parts = [ins[5 * p:5 * p + 5] for p in range(FINAL_PARTS)]
    h = jnp.concatenate([p[0][0] for p in parts], axis=0)
    c = jnp.concatenate([p[4][0] for p in parts], axis=0)
    hn = _residual(h, [_mix_chunk(p[1], p[2], p[3], 0) for p in parts], c, w_ref)
    o_ref[0] = hn * lax.rsqrt(jnp.mean(hn * hn, axis=-1, keepdims=True) + NORM_EPS) * g_ref[...]


def _final_call(h, oa, ob, z, c, w, g):
    B, Lq, D = h.shape
    S = Lq - LANE
    in_specs, args = [], []
    for p in range(FINAL_PARTS):
        chunk = lambda bb, i, p=p: (bb, FINAL_PARTS * i + p + 1)
        row_spec = lambda width, chunk=chunk: pl.BlockSpec((1, LANE, width), lambda bb, i: (*chunk(bb, i), 0))
        in_specs += [row_spec(D),
                     pl.BlockSpec((1, 1, FOX_W, LANE), lambda bb, i, chunk=chunk: (*chunk(bb, i), 0, 0)),
                     pl.BlockSpec((1, 1, DIFF_W, LANE), lambda bb, i, chunk=chunk: (*chunk(bb, i), 0, 0)),
                     row_spec(ATT_W), row_spec(POOL_W)]
        args += [h, oa, ob, z, c]
    in_specs += [pl.BlockSpec(w.shape, lambda bb, i: (0, 0)), pl.BlockSpec(g.shape, lambda bb, i: (0, 0))]
    return pl.pallas_call(
        _final_kernel,
        grid=(B, S // (FINAL_PARTS * LANE)),
        in_specs=in_specs,
        out_specs=pl.BlockSpec((1, FINAL_PARTS * LANE, D), lambda bb, i: (bb, i, 0)),
        out_shape=jax.ShapeDtypeStruct((B, S, D), F32),
        compiler_params=pltpu.CompilerParams(
            dimension_semantics=("arbitrary", "arbitrary"), vmem_limit_bytes=VMEM_LIMIT),
        name="out_proj_final",
    )(*args, w, g)


def _row_tile(lq):
    for tm in (384, 256, 128):
        if lq % tm == 0:
            return tm
    raise ValueError(f"sequence of {lq} rows is not a multiple of {LANE}")


def kernel(x, meta_tokens, norm_g, w_in, b_f, lam_q1, lam_k1, lam_q2, lam_k2, subln_g, w_pool,
           pool_scale, w_out, final_g):
    B, S, D = x.shape
    depth = w_in.shape[0]
    assert S % (QSUB * TQ) == 0 and D == HEAD_W and w_in.shape[2] == HEAD_W + FOX_HEADS + TAIL_W
    assert depth >= 2
    Lq = LANE + S
    tm = _row_tile(Lq)
    lead = jnp.concatenate([jnp.zeros((FRONT, D), x.dtype), meta_tokens.astype(x.dtype)], axis=0)
    w_all = _arrange_call(w_in)

    h = x
    prev = None
    for l in range(depth):
        lambda_init = 0.8 - 0.6 * math.exp(-0.3 * l)
        bfp = jnp.pad(b_f[l].astype(F32), (FORGET_LANE, 0))[None]
        wp = jax.scipy.linalg.block_diag(*[w_pool[l, g] for g in range(len(POOL_WINDOWS))]).astype(BF16)
        outs = _proj_call(h, norm_g[l][None].astype(F32), w_all, l, bfp, wp,
                          pool_scale[l][None].astype(F32), tm, prev, lead if l < 2 else None)
        if prev is not None:
            h, *outs = outs
        ka, qat, vt, z, cp = outs
        oa = _attn_call(ka, qat, vt, (), diff=False, lambda_init=lambda_init)
        extra = (lam_q1[l][None].astype(F32), lam_k1[l][None].astype(F32),
                 lam_q2[l][None].astype(F32), lam_k2[l][None].astype(F32),
                 subln_g[l][:, None].astype(F32))
        ob = _attn_call(ka, qat, vt, extra, diff=True, lambda_init=lambda_init)
        prev = (oa, ob, z, cp, w_out[l].astype(BF16))
    return _final_call(h, *prev, final_g[None].astype(F32))
```

```python
import functools
import math

import jax
import jax.numpy as jnp
from jax import lax
from jax.experimental import pallas as pl
from jax.experimental.pallas import tpu as pltpu

F32 = jnp.float32
BF16 = jnp.bfloat16

LANE = 128
N_META = 16
CHUNK = 64
FRONT = LANE - N_META
HEAD = 64
FOX_HEADS = 4
DIFF_HEADS = 4
N_INST = FOX_HEADS + 2 * DIFF_HEADS
FOX_W = 256
DIFF_W = 512
ATT_W = FOX_W + DIFF_W
POOL_W = 256
POOL_WINDOWS = (2, 4, 8, 16)
ALIBI_SLOPES = tuple(2.0 ** (-8.0 * (h + 1) / DIFF_HEADS) for h in range(DIFF_HEADS))
NORM_EPS = 1e-6
SUBLN_EPS = 1e-5
MASK = -1e30
M_INIT = -1e38
TQ = 256
QSUB = 4
CHAIN_KEYS = 128
NI = 4
GROUP_W = NI * HEAD
LOG2E = 1.4426950408889634
PROJ_BATCH = 2
FINAL_PARTS = 4
VMEM_LIMIT = 56 * 1024 * 1024

HEAD_W = 4 * FOX_W
TAIL_W = 4 * DIFF_W + 2 * POOL_W
TAIL_PAD = TAIL_W + LANE
W_COLS = HEAD_W + TAIL_PAD
FORGET_LANE = LANE - FOX_HEADS
C_FQ, C_FK, C_FV, C_FZ = 0, FOX_W, 2 * FOX_W, 3 * FOX_W
C_DQ, C_DK, C_DV, C_DZ = HEAD_W, HEAD_W + DIFF_W, HEAD_W + 2 * DIFF_W, HEAD_W + 3 * DIFF_W
C_PU, C_PZ, C_FG = HEAD_W + 4 * DIFF_W, HEAD_W + 4 * DIFF_W + POOL_W, HEAD_W + TAIL_W


def _split3(x):
    hi = x.astype(BF16)
    r1 = x - hi.astype(F32)
    mid = r1.astype(BF16)
    lo = (r1 - mid.astype(F32)).astype(BF16)
    return hi, mid, lo


def _arrange_kernel(w_ref, o_ref):
    rows = w_ref.shape[1]
    o_ref[0, :, 0:HEAD_W] = w_ref[0, :, 0:HEAD_W].astype(BF16)
    rot = pltpu.roll(w_ref[0, :, HEAD_W:HEAD_W + TAIL_W], TAIL_W - FOX_HEADS, axis=1)
    last = rot[:, TAIL_W - LANE:]
    lane = lax.broadcasted_iota(jnp.int32, (rows, LANE), 1)
    o_ref[0, :, C_FG:] = jnp.where(lane >= FORGET_LANE, last, 0.0).astype(BF16)
    end = w_ref[0, :, HEAD_W + TAIL_W:]
    for j in range(FOX_HEADS):
        last = jnp.where(lane == FORGET_LANE + j, end[:, j:j + 1], last)
    o_ref[0, :, HEAD_W:C_FG - LANE] = rot[:, 0:TAIL_W - LANE].astype(BF16)
    o_ref[0, :, C_FG - LANE:C_FG] = last.astype(BF16)


def _arrange_call(w_in):
    depth, d, n = w_in.shape
    rows = 256
    return pl.pallas_call(
        _arrange_kernel,
        grid=(depth, d // rows),
        in_specs=[pl.BlockSpec((1, rows, n), lambda l, i: (l, i, 0))],
        out_specs=pl.BlockSpec((1, rows, W_COLS), lambda l, i: (l, i, 0)),
        out_shape=jax.ShapeDtypeStruct((depth, d, W_COLS), BF16),
        compiler_params=pltpu.CompilerParams(
            dimension_semantics=("arbitrary", "arbitrary"), vmem_limit_bytes=VMEM_LIMIT),
        name="arrange_w",
    )(w_in)


def _proj_kernel(*refs, fused, x_chunks):
    i = pl.program_id(1)
    if x_chunks:
        x_refs, lead_ref, refs = refs[:x_chunks], refs[x_chunks], refs[x_chunks + 1:]
        nb, _, d = x_refs[0].shape
        tm = x_chunks * LANE
        first = jnp.where(i == 0, jnp.broadcast_to(lead_ref[...], (nb, LANE, d)), x_refs[0][...])
        h_in = jnp.concatenate([first] + [r[...] for r in x_refs[1:]], axis=1).reshape(nb * tm, d)
    else:
        h_ref, refs = refs[0], refs[1:]
        nb, tm, d = h_ref.shape
        h_in = h_ref[...].reshape(nb * tm, d)
    if fused:
        (oa_ref, ob_ref, zp_ref, cpp_ref, wo_ref, g_ref, w_ref, bf_ref, wp_ref, ps_ref,
         ho_ref, ka_ref, qat_ref, vt_ref, z_ref, cp_ref, carry_ref, u_ref) = refs
    else:
        (g_ref, w_ref, bf_ref, wp_ref, ps_ref,
         ka_ref, qat_ref, vt_ref, z_ref, cp_ref, carry_ref, u_ref) = refs
    halo = POOL_WINDOWS[-1]

    @pl.when(i == 0)
    def _():
        carry_ref[...] = jnp.zeros_like(carry_ref)
        u_ref[:, 0:halo, :] = jnp.zeros((nb, halo, POOL_W), F32)

    if fused:
        chunks = [_mix_chunk(oa_ref, ob_ref, zp_ref, s, bb) for bb in range(nb) for s in range(tm // LANE)]
        h = _residual(h_in, chunks, cpp_ref[...].reshape(nb * tm, POOL_W), wo_ref)
        ho_ref[...] = h.reshape(nb, tm, d)
    else:
        h = h_in
    hn = h * lax.rsqrt(jnp.mean(h * h, axis=-1, keepdims=True) + NORM_EPS) * g_ref[...]
    hnb = hn.astype(BF16)
    fg_all = jnp.dot(hnb, w_ref[0, :, C_FG:], preferred_element_type=F32)
    proj_all = jnp.dot(hnb, w_ref[0, :, 0:C_FG], preferred_element_type=F32)

    row = i * tm + lax.broadcasted_iota(jnp.int32, (tm, LANE), 0)
    tri = (lax.broadcasted_iota(jnp.int32, (LANE, LANE), 0)
           >= lax.broadcasted_iota(jnp.int32, (LANE, LANE), 1)).astype(BF16)
    lane = lax.broadcasted_iota(jnp.int32, (LANE, LANE), 1)
    lm = lane & (HEAD - 1)
    low_half = lane < HEAD
    rloc = lax.broadcasted_iota(jnp.int32, (LANE, 1), 0)
    prow = i * tm + lax.broadcasted_iota(jnp.int32, (tm, POOL_W), 0)
    plane = lax.broadcasted_iota(jnp.int32, (tm, POOL_W), 1)
    pos1 = prow - (FRONT - 1)
    gd = POOL_W // len(POOL_WINDOWS)

    def extras(q_bias, dead):
        qh, qm, ql = (p.astype(F32) for p in _split3(q_bias))
        one_q = jnp.where((lm >= 3) & (lm < 7), 1.0, 0.0)
        eq = jnp.where(lm == 0, qh, jnp.where(lm == 1, qm, jnp.where(lm == 2, ql, one_q)))
        one_k = jnp.where(lm < 3, 1.0, 0.0)
        ek = jnp.where(lm == 3, -qh, jnp.where(lm == 4, -qm, jnp.where(lm == 5, -ql,
                                                                       jnp.where(lm == 6, dead, one_k))))
        return eq, ek

    for bb in range(nb):
        proj = proj_all[bb * tm:(bb + 1) * tm]
        fg = fg_all[bb * tm:(bb + 1) * tm]
        z_ref[bb, :, 0:FOX_W] = proj[:, C_FZ:C_FZ + FOX_W].astype(BF16)
        z_ref[bb, :, FOX_W:] = proj[:, C_DZ:C_DZ + DIFF_W].astype(BF16)
        pu = proj[:, C_PU:C_PU + POOL_W]
        pz = proj[:, C_PZ:C_PZ + POOL_W]

        logf = jnp.where(row >= FRONT, jax.nn.log_sigmoid(fg + bf_ref[...]), 0.0)
        carry = carry_ref[bb, 0:1, :]
        for s in range(tm // LANE):
            rs = slice(s * LANE, (s + 1) * LANE)
            parts = _split3(logf[rs])
            within = sum(jnp.dot(tri, p, preferred_element_type=F32) for p in parts)
            cblk = within + carry
            carry = cblk[LANE - 1:LANE, :]

            grow = i * tm + s * LANE + rloc
            dead = jnp.where(grow < FRONT, MASK, 0.0)
            for hd in range(FOX_HEADS):
                cb = LOG2E * cblk[:, FORGET_LANE + hd:FORGET_LANE + hd + 1]
                eq, ek = extras(cb, dead)
                grp = slice((hd // 2) * LANE, (hd // 2 + 1) * LANE)
                own = low_half if hd % 2 == 0 else jnp.logical_not(low_half)
                q = proj[rs, C_FQ:C_FQ + FOX_W][:, grp] * (HEAD ** -0.5 * LOG2E)
                k = proj[rs, C_FK:C_FK + FOX_W][:, grp]
                ka_ref[bb, hd, rs, :] = jnp.where(own, k, ek).astype(BF16)
                qat_ref[bb, hd, s] = jnp.where(own, q, eq).T.astype(BF16)
            for hd in range(DIFF_HEADS):
                tb = (-LOG2E * ALIBI_SLOPES[hd]) * grow.astype(F32)
                eq, ek = extras(tb, dead)
                grp = slice(hd * LANE, (hd + 1) * LANE)
                q = proj[rs, C_DQ:C_DQ + DIFF_W][:, grp] * (HEAD ** -0.5 * LOG2E)
                k = proj[rs, C_DK:C_DK + DIFF_W][:, grp]
                for c in range(2):
                    own = low_half if c == 0 else jnp.logical_not(low_half)
                    inst = FOX_HEADS + 2 * hd + c
                    ka_ref[bb, inst, rs, :] = jnp.where(own, k, ek).astype(BF16)
                    qat_ref[bb, inst, s] = jnp.where(own, q, eq).T.astype(BF16)
            for g in range(FOX_W // LANE):
                vt_ref[bb, s, g * LANE:(g + 1) * LANE, :] = (
                    proj[rs, C_FV + g * LANE:C_FV + (g + 1) * LANE].T.astype(BF16))
            for g in range(DIFF_W // LANE):
                vt_ref[bb, s, FOX_W + g * LANE:FOX_W + (g + 1) * LANE, :] = (
                    proj[rs, C_DV + g * LANE:C_DV + (g + 1) * LANE].T.astype(BF16))
        carry_ref[bb] = jnp.broadcast_to(carry, carry_ref.shape[1:])

        u_ref[bb, halo:halo + tm, :] = pu
        acc = pu
        sums = {}
        for k in range(1, halo):
            acc = acc + u_ref[bb, halo - k:halo - k + tm, :]
            if k + 1 in POOL_WINDOWS:
                sums[k + 1] = acc
        mean = None
        for gi, wdw in reversed(list(enumerate(POOL_WINDOWS))):
            cand = sums[wdw] / jnp.clip(pos1, 1, wdw).astype(F32)
            mean = cand if mean is None else jnp.where(plane < (gi + 1) * gd, cand, mean)
        pooled = mean - pu
        y = jnp.dot(pooled.astype(BF16), wp_ref[...], preferred_element_type=F32)
        cp_ref[bb] = (y * ps_ref[...] * jax.nn.silu(pz)).astype(BF16)
        u_ref[bb, 0:halo, :] = pu[tm - halo:tm, :]


def _proj_call(h, g, w_all, layer, bfp, wp, ps, tm, prev=None, lead=None):
    B, rows_in, D = h.shape
    from_x = lead is not None
    Lq = rows_in + LANE if from_x else rows_in
    fused = prev is not None
    nb = 1 if B % PROJ_BATCH else PROJ_BATCH
    nt = Lq // tm
    cpt = tm // LANE
    row_spec = lambda width: pl.BlockSpec((nb, tm, width), lambda b, i: (b, i, 0))
    full = lambda a: pl.BlockSpec(a.shape, lambda b, i: (0,) * a.ndim)
    if from_x:
        in_specs = [pl.BlockSpec((nb, LANE, D), lambda b, i, j=j: (b, jnp.maximum(cpt * i + j - 1, 0), 0))
                    for j in range(cpt)] + [full(lead)]
        args = [h] * cpt + [lead]
    else:
        in_specs = [row_spec(D)]
        args = [h]
    out_specs, out_shape = [], []
    if fused:
        oa, ob, zp, cpp, wo = prev
        in_specs += [pl.BlockSpec((nb, cpt, FOX_W, LANE), lambda b, i: (b, i, 0, 0)),
                     pl.BlockSpec((nb, cpt, DIFF_W, LANE), lambda b, i: (b, i, 0, 0)),
                     row_spec(ATT_W), row_spec(POOL_W), full(wo)]
        args += [oa, ob, zp, cpp, wo]
        out_specs.append(row_spec(D))
        out_shape.append(jax.ShapeDtypeStruct((B, Lq, D), F32))
    in_specs += [full(g), pl.BlockSpec((1, D, W_COLS), lambda b, i: (layer, 0, 0)),
                 full(bfp), full(wp), full(ps)]
    args += [g, w_all, bfp, wp, ps]
    out_specs += [pl.BlockSpec((nb, N_INST, tm, LANE), lambda b, i: (b, 0, i, 0)),
                  pl.BlockSpec((nb, N_INST, cpt, LANE, LANE), lambda b, i: (b, 0, i, 0, 0)),
                  pl.BlockSpec((nb, cpt, ATT_W, LANE), lambda b, i: (b, i, 0, 0)),
                  row_spec(ATT_W), row_spec(POOL_W)]
    out_shape += [jax.ShapeDtypeStruct((B, N_INST, Lq, LANE), BF16),
                  jax.ShapeDtypeStruct((B, N_INST, Lq // LANE, LANE, LANE), BF16),
                  jax.ShapeDtypeStruct((B, Lq // LANE, ATT_W, LANE), BF16),
                  jax.ShapeDtypeStruct((B, Lq, ATT_W), BF16),
                  jax.ShapeDtypeStruct((B, Lq, POOL_W), BF16)]
    return pl.pallas_call(
        functools.partial(_proj_kernel, fused=fused, x_chunks=cpt if from_x else 0),
        grid=(B // nb, nt),
        in_specs=in_specs,
        out_specs=out_specs,
        out_shape=out_shape,
        scratch_shapes=[pltpu.VMEM((nb, 8, LANE), F32),
                        pltpu.VMEM((nb, POOL_WINDOWS[-1] + tm, POOL_W), F32)],
        compiler_params=pltpu.CompilerParams(
            dimension_semantics=("arbitrary", "arbitrary"), vmem_limit_bytes=VMEM_LIMIT),
        name="layer" if fused else "proj",
    )(*args)


def _attn_kernel(*refs, diff, lambda_init):
    if diff:
        ka_ref, qat_ref, vt_ref, lq1_ref, lk1_ref, lq2_ref, lk2_ref, sg_ref, ot_ref, dm_ref = refs
    else:
        ka_ref, qat_ref, vt_ref, ot_ref, dm_ref = refs
    lq = ka_ref.shape[2]
    n_sblocks = (lq - LANE) // (QSUB * TQ)
    pid = pl.program_id(1)
    dv = LANE if diff else HEAD
    kc = TQ // LANE

    kk = lax.broadcasted_iota(jnp.int32, (TQ, TQ), 0)
    qq = lax.broadcasted_iota(jnp.int32, (TQ, TQ), 1)
    if diff:
        for hh in range(NI // 2):
            slope = jnp.where(pid == 0, ALIBI_SLOPES[hh], ALIBI_SLOPES[NI // 2 + hh]).astype(F32)
            fix = jnp.where(kk > qq, (2.0 * LOG2E * slope) * (qq - kk).astype(F32), 0.0)
            dm_ref[hh] = jnp.where(kk // CHUNK > qq // CHUNK, MASK, fix)
        lam = (jnp.exp(jnp.sum(lq1_ref[...] * lk1_ref[...], axis=-1, keepdims=True))
               - jnp.exp(jnp.sum(lq2_ref[...] * lk2_ref[...], axis=-1, keepdims=True))
               + lambda_init)
    else:
        dm_ref[0] = jnp.where(kk > qq, MASK, 0.0)

    def vrows(inst):
        return slice(LANE * (inst // 2), LANE * (inst // 2 + 1))

    def dm(inst, rows=slice(None), cols=slice(None)):
        return dm_ref[inst // 2 if diff else 0, rows, cols]

    def scores(inst, qat, key_start, n_key_chunks):
        ka = ka_ref[0, inst, pl.ds(key_start, n_key_chunks * LANE), :]
        return jnp.dot(ka, qat, preferred_element_type=F32)

    def partial(inst, s, first_chunk):
        m = jnp.max(s, axis=0, keepdims=True)
        p = jnp.exp2(s - m)
        l = jnp.sum(p, axis=0, keepdims=True)
        pb = p.astype(BF16)
        pv = None
        for cch in range(s.shape[0] // LANE):
            vt = vt_ref[0, first_chunk + cch, vrows(inst), :]
            d = jnp.dot(vt, pb[cch * LANE:(cch + 1) * LANE], preferred_element_type=F32)
            pv = d if pv is None else pv + d
        if not diff:
            pv = pv[HEAD * (inst % 2):HEAD * (inst % 2 + 1)]
        return m, l, pv

    def merge(state, parts):
        m, l, acc = state
        m_new = m
        for mp, _, _ in parts:
            m_new = jnp.maximum(m_new, mp)
        w = jnp.exp2(m - m_new)
        l = w * l
        acc = w * acc
        for mp, lp, pvp in parts:
            wp = jnp.exp2(mp - m_new)
            l = l + wp * lp
            acc = acc + wp * pvp
        return m_new, l, acc

    def init_state(tq):
        return (jnp.full((1, tq), M_INIT, F32), jnp.zeros((1, tq), F32), jnp.zeros((dv, tq), F32))

    def finish(states, first_chunk, n_q_chunks):
        if diff:
            heads = []
            for hh in range(NI // 2):
                (_, l0, a0), (_, l1, a1) = states[2 * hh], states[2 * hh + 1]
                o = a0 / l0 - lam * (a1 / l1)
                o = o * lax.rsqrt(jnp.mean(o * o, axis=0, keepdims=True) + SUBLN_EPS) * sg_ref[...]
                heads.append(o * (1.0 - lambda_init))
        else:
            heads = [acc / l for (_, l, acc) in states]
        o = jnp.concatenate(heads, axis=0)
        for cch in range(n_q_chunks):
            ot_ref[0, first_chunk + cch] = o[:, cch * LANE:(cch + 1) * LANE]

    lead = []
    for inst in range(NI):
        s = scores(inst, qat_ref[0, inst, 0], 0, 1) + dm(inst, slice(0, LANE), slice(0, LANE))
        lead.append(merge(init_state(LANE), [partial(inst, s, 0)]))
    finish(lead, 0, 1)

    pairs = [(inst, qs) for inst in range(NI) for qs in range(QSUB)]

    def superblock(i, _):
        qats = {}
        for inst, qs in pairs:
            c0 = 1 + kc * (QSUB * i + qs)
            qats[inst, qs] = jnp.concatenate([qat_ref[0, inst, c0 + c] for c in range(kc)], axis=1)

        def kvstep(j, carry):
            k_start = pl.multiple_of(LANE + j * (QSUB * TQ), LANE)
            out = []
            for n, (inst, qs) in enumerate(pairs):
                s = scores(inst, qats[inst, qs], k_start, QSUB * kc)
                parts = [partial(inst, s[u * CHAIN_KEYS:(u + 1) * CHAIN_KEYS],
                                 1 + kc * QSUB * j + u * (CHAIN_KEYS // LANE))
                         for u in range(QSUB * TQ // CHAIN_KEYS)]
                out.extend(merge(carry[3 * n:3 * n + 3], parts))
            return tuple(out)

        carry0 = tuple(x for _ in pairs for x in init_state(TQ))
        carry = lax.fori_loop(0, i, kvstep, carry0)

        d_start = pl.multiple_of(LANE + i * (QSUB * TQ), LANE)
        for qs in range(QSUB):
            states = []
            for inst in range(NI):
                n = pairs.index((inst, qs))
                parts = [partial(inst, scores(inst, qats[inst, qs], 0, 1), 0)]
                s = scores(inst, qats[inst, qs], d_start, (qs + 1) * kc)
                for u in range(qs + 1):
                    su = s[u * TQ:(u + 1) * TQ]
                    if u == qs:
                        su = su + dm(inst)
                    parts.append(partial(inst, su, 1 + kc * (QSUB * i + u)))
                states.append(merge(carry[3 * n:3 * n + 3], parts))
            finish(states, 1 + kc * (QSUB * i + qs), kc)
        return 0

    lax.fori_loop(0, n_sblocks, superblock, 0)


def _attn_call(ka, qat, vt, extra, *, diff, lambda_init):
    B, _, Lq, _ = ka.shape
    nc = Lq // LANE
    n_groups = 2 * DIFF_HEADS // NI if diff else FOX_HEADS // NI
    first = FOX_HEADS // NI if diff else 0
    in_specs = [pl.BlockSpec((1, NI, Lq, LANE), lambda b, g: (b, first + g, 0, 0)),
                pl.BlockSpec((1, NI, nc, LANE, LANE), lambda b, g: (b, first + g, 0, 0, 0)),
                pl.BlockSpec((1, nc, GROUP_W, LANE), lambda b, g: (b, 0, first + g, 0))]
    args = [ka, qat, vt]
    for a in extra:
        in_specs.append(pl.BlockSpec(a.shape, lambda b, g, nd=a.ndim: (0,) * nd))
        args.append(a)
    return pl.pallas_call(
        functools.partial(_attn_kernel, diff=diff, lambda_init=lambda_init),
        grid=(B, n_groups),
        in_specs=in_specs,
        out_specs=pl.BlockSpec((1, nc, GROUP_W, LANE), lambda b, g: (b, 0, g, 0)),
        out_shape=jax.ShapeDtypeStruct((B, nc, n_groups * GROUP_W, LANE), F32),
        scratch_shapes=[pltpu.VMEM((NI // 2 if diff else 1, TQ, TQ), F32)],
        compiler_params=pltpu.CompilerParams(
            dimension_semantics=("arbitrary", "arbitrary"), vmem_limit_bytes=VMEM_LIMIT),
        name="diff_attn" if diff else "fox_attn",
    )(*args)


def _mix_chunk(oa_ref, ob_ref, z_ref, s, bb=0):
    pieces = [oa_ref[bb, s, g * LANE:(g + 1) * LANE, :].T for g in range(FOX_W // LANE)]
    pieces += [ob_ref[bb, s, g * LANE:(g + 1) * LANE, :].T for g in range(DIFF_W // LANE)]
    att = jnp.concatenate(pieces, axis=1)
    gate = jax.nn.silu(z_ref[bb, s * LANE:(s + 1) * LANE, :].astype(F32))
    return (att * gate).astype(BF16)


def _residual(h, mix_chunks, c, w_ref):
    mix = mix_chunks[0] if len(mix_chunks) == 1 else jnp.concatenate(mix_chunks, axis=0)
    return (h + jnp.dot(mix, w_ref[0:ATT_W, :], preferred_element_type=F32)
            + jnp.dot(c, w_ref[ATT_W:, :], preferred_element_type=F32))


def _final_kernel(*refs):
    ins, (w_ref, g_ref, o_ref) = refs[:-3], refs[-3:]
    parts = [ins[5 * p:5 * p + 5] for p in range(FINAL_PARTS)]
    h = jnp.concatenate([p[0][0] for p in parts], axis=0)
    c = jnp.concatenate([p[4][0] for p in parts], axis=0)
    hn = _residual(h, [_mix_chunk(p[1], p[2], p[3], 0) for p in parts], c, w_ref)
    o_ref[0] = hn * lax.rsqrt(jnp.mean(hn * hn, axis=-1, keepdims=True) + NORM_EPS) * g_ref[...]


def _final_call(h, oa, ob, z, c, w, g):
    B, Lq, D = h.shape
    S = Lq - LANE
    in_specs, args = [], []
    for p in range(FINAL_PARTS):
        chunk = lambda bb, i, p=p: (bb, FINAL_PARTS * i + p + 1)
        row_spec = lambda width, chunk=chunk: pl.BlockSpec((1, LANE, width), lambda bb, i: (*chunk(bb, i), 0))
        in_specs += [row_spec(D),
                     pl.BlockSpec((1, 1, FOX_W, LANE), lambda bb, i, chunk=chunk: (*chunk(bb, i), 0, 0)),
                     pl.BlockSpec((1, 1, DIFF_W, LANE), lambda bb, i, chunk=chunk: (*chunk(bb, i), 0, 0)),
                     row_spec(ATT_W), row_spec(POOL_W)]
        args += [h, oa, ob, z, c]
    in_specs += [pl.BlockSpec(w.shape, lambda bb, i: (0, 0)), pl.BlockSpec(g.shape, lambda bb, i: (0, 0))]
    return pl.pallas_call(
        _final_kernel,
        grid=(B, S // (FINAL_PARTS * LANE)),
        in_specs=in_specs,
        out_specs=pl.BlockSpec((1, FINAL_PARTS * LANE, D), lambda bb, i: (bb, i, 0)),
        out_shape=jax.ShapeDtypeStruct((B, S, D), F32),
        compiler_params=pltpu.CompilerParams(
            dimension_semantics=("arbitrary", "arbitrary"), vmem_limit_bytes=VMEM_LIMIT),
        name="out_proj_final",
    )(*args, w, g)


def _row_tile(lq):
    for tm in (384, 256, 128):
        if lq % tm == 0:
            return tm
    raise ValueError(f"sequence of {lq} rows is not a multiple of {LANE}")


def kernel(x, meta_tokens, norm_g, w_in, b_f, lam_q1, lam_k1, lam_q2, lam_k2, subln_g, w_pool,
           pool_scale, w_out, final_g):
    B, S, D = x.shape
    depth = w_in.shape[0]
    assert S % (QSUB * TQ) == 0 and D == HEAD_W and w_in.shape[2] == HEAD_W + FOX_HEADS + TAIL_W
    assert depth >= 2
    Lq = LANE + S
    tm = _row_tile(Lq)
    lead = jnp.concatenate([jnp.zeros((FRONT, D), x.dtype), meta_tokens.astype(x.dtype)], axis=0)
    w_all = _arrange_call(w_in)

    h = x
    prev = None
    for l in range(depth):
        lambda_init = 0.8 - 0.6 * math.exp(-0.3 * l)
        bfp = jnp.pad(b_f[l].astype(F32), (FORGET_LANE, 0))[None]
        wp = jax.scipy.linalg.block_diag(*[w_pool[l, g] for g in range(len(POOL_WINDOWS))]).astype(BF16)
        outs = _proj_call(h, norm_g[l][None].astype(F32), w_all, l, bfp, wp,
                          pool_scale[l][None].astype(F32), tm, prev, lead if l < 2 else None)
        if prev is not None:
            h, *outs = outs
        ka, qat, vt, z, cp = outs
        oa = _attn_call(ka, qat, vt, (), diff=False, lambda_init=lambda_init)
        extra = (lam_q1[l][None].astype(F32), lam_k1[l][None].astype(F32),
                 lam_q2[l][None].astype(F32), lam_k2[l][None].astype(F32),
                 subln_g[l][:, None].astype(F32))
        ob = _attn_call(ka, qat, vt, extra, diff=True, lambda_init=lambda_init)
        prev = (oa, ob, z, cp, w_out[l].astype(BF16))
    return _final_call(h, *prev, final_g[None].astype(F32))
```
